```python
import math
import numpy as np
import jax
import jax.numpy as jnp
from jax import lax

D_MODEL = 1024
BATCH = 8
SEQ = 2048
DEPTH = 4
DEC_BATCH = 32
DEC_SEQ = 4
PAST_LEN = 8192
PAGE_SIZE = 128

C_CONV = 512
CONV_W = 31
N_HEADS = 8
N_KV = 2
HG = N_HEADS // N_KV
HEAD_DIM = 64
CMP_LEN = 32
CMP_STRIDE = 16
CMP_HID = 64
SLC_BLK = 64
N_SEL = 16
WINDOW = 512
ROPE_THETA = 10000.0
WIN_QBLK = 128
SLC_QBLK = 64
D_FF = ((-(-8 * D_MODEL // 3) + 255) // 256) * 256
ALPHA = (2.0 * DEPTH) ** 0.25
BETA = (8.0 * DEPTH) ** -0.25
LN_EPS = 1e-5
MASK_VALUE = -1e30
FORCE_SCORE = 1e4
KV_W = N_KV * HEAD_DIM
SPLIT_SIZES = (C_CONV, C_CONV, N_HEADS * HEAD_DIM, KV_W, KV_W, KV_W, KV_W, KV_W, KV_W, 3 * N_HEADS, D_MODEL, D_MODEL)
D_IN = sum(SPLIT_SIZES)
SCALE = HEAD_DIM ** -0.5

kernel_name = 'hybrid_conformer_nsa_decode_step'


def layer_norm(x, g, b):
    xf = x.astype(jnp.float32)
    mu = jnp.mean(xf, -1, keepdims=True)
    var = jnp.mean(jnp.square(xf - mu), -1, keepdims=True)
    y = (xf - mu) * lax.rsqrt(var + LN_EPS)
    return (y * g.astype(jnp.float32) + b.astype(jnp.float32)).astype(x.dtype)


def rope(x, pos):
    half = HEAD_DIM // 2
    inv = ROPE_THETA ** (-jnp.arange(half, dtype=jnp.float32) / half)
    ang = pos.astype(jnp.float32)[:, None] * inv[None, :]
    shape = (1, pos.shape[0]) + (1,) * (x.ndim - 3) + (half,)
    cos = jnp.cos(ang).reshape(shape)
    sin = jnp.sin(ang).reshape(shape)
    xf = x.astype(jnp.float32)
    x1, x2 = xf[..., :half], xf[..., half:]
    return jnp.concatenate([x1 * cos - x2 * sin, x2 * cos + x1 * sin], -1).astype(x.dtype)


def compress(rows, pe, w1, w2):
    b, length, g, d = rows.shape
    r = CMP_LEN // CMP_STRIDE
    n_cmp = (length - CMP_LEN) // CMP_STRIDE + 1
    n_ch = n_cmp + r - 1
    ch = rows[:, :n_ch * CMP_STRIDE].reshape(b, n_ch, CMP_STRIDE, g, d)
    pe_r = pe.reshape(r, CMP_STRIDE, d)
    w1_r = w1.reshape(r, CMP_STRIDE, d, CMP_HID)
    h = jnp.einsum('bnsgd,sde->bnge', ch[:, :n_cmp] + pe_r[0][:, None, :], w1_r[0])
    for m in range(1, r):
        h = h + jnp.einsum('bnsgd,sde->bnge', ch[:, m:m + n_cmp] + pe_r[m][:, None, :], w1_r[m])
    return jnp.einsum('bnge,ed->bngd', jax.nn.gelu(h), w2)


def cmp_attention(q, kc, vc, q_pos):
    n = kc.shape[1]
    ends = jnp.arange(n) * CMP_STRIDE + (CMP_LEN - 1)
    mask = ends[None, :] <= q_pos[:, None]
    s = jnp.einsum('btghd,bngd->bghtn', q, kc).astype(jnp.float32) * SCALE
    p = jax.nn.softmax(jnp.where(mask, s, MASK_VALUE), axis=-1)
    p = p * jnp.any(mask, -1)[:, None].astype(jnp.float32)
    o = jnp.einsum('bghtn,bngd->btghd', p.astype(vc.dtype), vc)
    return o, p


def select_blocks(p_cmp, q_pos, n_slc):
    n_cmp = p_cmp.shape[-1]
    i = np.arange(n_cmp)[:, None]
    j = np.arange(n_slc)[None, :]
    overlap = np.minimum(i * CMP_STRIDE + CMP_LEN, (j + 1) * SLC_BLK) - np.maximum(i * CMP_STRIDE, j * SLC_BLK)
    m = jnp.asarray(np.clip(overlap, 0, None) / CMP_LEN, dtype=jnp.float32)
    score = jnp.einsum('bghtn,nj->bgtj', p_cmp, m)
    cur = (q_pos // SLC_BLK)[:, None]
    blk = jnp.arange(n_slc)[None, :]
    forced = (blk == 0) | (blk == cur) | (blk == cur - 1)
    score = jnp.where(forced, FORCE_SCORE, jnp.where(blk <= cur, score, -1.0))
    _, idx = lax.top_k(score, min(N_SEL, n_slc))
    return idx


def slc_attention(q, k_rows, v_rows, q_pos, idx):
    b, length, g, d = k_rows.shape
    t = q.shape[1]
    n_slc = -(-length // SLC_BLK)
    pad = n_slc * SLC_BLK - length

    def blocks(r):
        r = jnp.pad(r, ((0, 0), (0, pad), (0, 0), (0, 0)))
        return r.reshape(b, n_slc, SLC_BLK, g, d).transpose(0, 3, 1, 2, 4)

    kb, vb = blocks(k_rows), blocks(v_rows)
    qb_len = math.gcd(t, SLC_QBLK)
    nq = t // qb_len
    k_sel = idx.shape[-1]
    qs = q.reshape(b, nq, qb_len, g, HG, d).transpose(1, 0, 2, 3, 4, 5)
    ids = idx.reshape(b, g, nq, qb_len, k_sel).transpose(2, 0, 1, 3, 4)
    ps = q_pos.reshape(nq, qb_len)
    bi = jnp.arange(b)[:, None, None, None]
    gi = jnp.arange(g)[None, :, None, None]

    def one_block(args):
        qq, ii, pp = args
        kg = kb[bi, gi, ii]
        vg = vb[bi, gi, ii]
        kpos = ii[..., None] * SLC_BLK + jnp.arange(SLC_BLK)
        mask = (kpos <= pp[None, None, :, None, None])[:, :, None]
        s = jnp.einsum('btghd,bgtkld->bghtkl', qq, kg).astype(jnp.float32) * SCALE
        s = jnp.where(mask, s, MASK_VALUE)
        sh = s.shape
        p = jax.nn.softmax(s.reshape(sh[:4] + (-1,)), axis=-1).reshape(sh)
        return jnp.einsum('bghtkl,bgtkld->btghd', p.astype(vg.dtype), vg)

    o = lax.map(one_block, (qs, ids, ps))
    return o.transpose(1, 0, 2, 3, 4, 5).reshape(b, t, g, HG, d)


def win_attention_prompt(q, k, v):
    b, t, g, d = k.shape
    wb = math.gcd(t, WIN_QBLK)
    nb = t // wb
    nband = -(-WINDOW // wb)

    def band(r):
        rp = jnp.pad(r, ((0, 0), (nband * wb, 0), (0, 0), (0, 0))).reshape(b, nb + nband, wb, g, d)
        return jnp.concatenate([rp[:, m:m + nb] for m in range(nband + 1)], axis=2)

    kband, vband = band(k), band(v)
    qb = q.reshape(b, nb, wb, g, HG, d)
    qpos = jnp.arange(t).reshape(nb, wb)
    kpos = jnp.arange(nb)[:, None] * wb + jnp.arange(-nband * wb, wb)[None, :]
    diff = qpos[:, :, None] - kpos[:, None, :]
    mask = (diff >= 0) & (diff < WINDOW) & (kpos[:, None, :] >= 0)
    s = jnp.einsum('bnqghd,bnkgd->bghnqk', qb, kband).astype(jnp.float32) * SCALE
    p = jax.nn.softmax(jnp.where(mask, s, MASK_VALUE), axis=-1)
    o = jnp.einsum('bghnqk,bnkgd->bnqghd', p.astype(vband.dtype), vband)
    return o.reshape(b, t, g, HG, d)


def win_attention_dense(q, k, v, q_pos, k_pos):
    diff = q_pos[:, None] - k_pos[None, :]
    mask = (diff >= 0) & (diff < WINDOW)
    s = jnp.einsum('btghd,bsgd->bghts', q, k).astype(jnp.float32) * SCALE
    p = jax.nn.softmax(jnp.where(mask, s, MASK_VALUE), axis=-1)
    return jnp.einsum('bghts,bsgd->btghd', p.astype(v.dtype), v)


def token_mixer(x, start, past, w_in, cmp_pe_k, cmp_w1_k, cmp_w2_k, cmp_pe_v, cmp_w1_v, cmp_w2_v,
                conv_w, conv_b, conv_ln_g, conv_ln_b, w_conv_out, w_attn_out, w_o):
    b, t, _ = x.shape
    q_pos = start + jnp.arange(t)
    cuts = np.cumsum(SPLIT_SIZES)[:-1].tolist()
    (ua, ug, q, kc, vc, ks, vs, kw, vw, g_nsa, g_a, g_b) = jnp.split(x @ w_in, cuts, axis=-1)

    u = ua * jax.nn.sigmoid(ug)
    prev = jnp.zeros((b, CONV_W - 1, C_CONV), u.dtype) if past is None else past[0]
    u_ext = jnp.concatenate([prev, u], axis=1)
    conv = lax.conv_general_dilated(u_ext, conv_w[:, None, :], (1,), 'VALID',
                                    dimension_numbers=('NWC', 'WIO', 'NWC'),
                                    feature_group_count=C_CONV) + conv_b
    conv_out = jax.nn.silu(layer_norm(conv, conv_ln_g, conv_ln_b)) @ w_conv_out

    heads = lambda z: z.reshape(b, t, N_KV, HEAD_DIM)
    q = q.reshape(b, t, N_KV, HG, HEAD_DIM)
    q_rot = rope(q, q_pos)
    kc, vc, vs, vw = heads(kc), heads(vc), heads(vs), heads(vw)
    ks = rope(heads(ks), q_pos)
    kw = rope(heads(kw), q_pos)
    if past is None:
        kc_all, vc_all, ks_all, vs_all = kc, vc, ks, vs
        o_win = win_attention_prompt(q_rot, kw, vw)
        n_keep = min(WINDOW, t)
        new_kw, new_vw = kw[:, t - n_keep:], vw[:, t - n_keep:]
    else:
        _, pkc, pvc, pks, pvs, pkw, pvw = past
        cat = lambda a, z: jnp.concatenate([a, z], axis=1)
        kc_all, vc_all, ks_all, vs_all = cat(pkc, kc), cat(pvc, vc), cat(pks, ks), cat(pvs, vs)
        kw_all, vw_all = cat(pkw, kw), cat(pvw, vw)
        wbuf = pkw.shape[1]
        k_pos = start - wbuf + jnp.arange(wbuf + t)
        o_win = win_attention_dense(q_rot, kw_all, vw_all, q_pos, k_pos)
        new_kw, new_vw = kw_all[:, t:], vw_all[:, t:]
    k_cmp = compress(kc_all, cmp_pe_k, cmp_w1_k, cmp_w2_k)
    v_cmp = compress(vc_all, cmp_pe_v, cmp_w1_v, cmp_w2_v)
    o_cmp, p_cmp = cmp_attention(q, k_cmp, v_cmp, q_pos)
    idx = select_blocks(p_cmp, q_pos, -(-kc_all.shape[1] // SLC_BLK))
    o_slc = slc_attention(q_rot, ks_all, vs_all, q_pos, idx)
    gates = jax.nn.sigmoid(g_nsa.astype(jnp.float32)).reshape(b, t, N_KV, HG, 3).astype(x.dtype)
    o = gates[..., 0:1] * o_cmp + gates[..., 1:2] * o_slc + gates[..., 2:3] * o_win
    attn_out = o.reshape(b, t, N_HEADS * HEAD_DIM) @ w_attn_out

    mixed = jax.nn.sigmoid(g_a) * conv_out + jax.nn.sigmoid(g_b) * attn_out
    new_state = (u_ext[:, -(CONV_W - 1):], kc, vc, ks, vs, new_kw, new_vw)
    return mixed @ w_o, new_state


def decoder_layer(x, start, past, mix, ffn):
    ln1_g, ln1_b, w_gate, w_up, w_down, ln2_g, ln2_b = ffn
    y, state = token_mixer(x, start, past, *mix)
    x = layer_norm(ALPHA * x + y, ln1_g, ln1_b)
    h = jax.nn.silu(x @ w_gate) * (x @ w_up)
    x = layer_norm(ALPHA * x + h @ w_down, ln2_g, ln2_b)
    return x, state


def gather_pages(pool, page_table):
    dbatch, n_pages = page_table.shape
    return pool[page_table].reshape(dbatch, n_pages * pool.shape[1], N_KV, HEAD_DIM)


def setup_inputs(seed: int = 0) -> dict:
    key = jax.random.key(seed)
    keys = list(jax.random.split(key, 40))
    nrm = lambda shape, scale: jax.random.normal(keys.pop(), shape, jnp.float32) * scale
    n_pages = PAST_LEN // PAGE_SIZE
    n_pool = (5 * DEC_BATCH * n_pages + 3) // 4
    wbuf = min(WINDOW, PAST_LEN)
    paged = (DEPTH, n_pool, PAGE_SIZE, N_KV, HEAD_DIM)
    perm = jax.random.permutation(keys.pop(), n_pool)[:DEC_BATCH * n_pages]
    return {
        'x_prompt': nrm((BATCH, SEQ, D_MODEL), 1.0),
        'x_sample': nrm((DEC_BATCH, DEC_SEQ, D_MODEL), 1.0),
        'cache_cmp_k': nrm(paged, 1.0),
        'cache_cmp_v': nrm(paged, 1.0),
        'cache_slc_k': nrm(paged, 1.0),
        'cache_slc_v': nrm(paged, 1.0),
        'state_win_k': nrm((DEPTH, DEC_BATCH, wbuf, N_KV, HEAD_DIM), 1.0),
        'state_win_v': nrm((DEPTH, DEC_BATCH, wbuf, N_KV, HEAD_DIM), 1.0),
        'state_conv': nrm((DEPTH, DEC_BATCH, CONV_W - 1, C_CONV), 0.5),
        'page_table': perm.reshape(DEC_BATCH, n_pages).astype(jnp.int32),
        'w_in': nrm((DEPTH, D_MODEL, D_IN), D_MODEL ** -0.5),
        'cmp_pe_k': nrm((DEPTH, CMP_LEN, HEAD_DIM), 0.2),
        'cmp_w1_k': nrm((DEPTH, CMP_LEN, HEAD_DIM, CMP_HID), (CMP_LEN * HEAD_DIM) ** -0.5),
        'cmp_w2_k': nrm((DEPTH, CMP_HID, HEAD_DIM), CMP_HID ** -0.5),
        'cmp_pe_v': nrm((DEPTH, CMP_LEN, HEAD_DIM), 0.2),
        'cmp_w1_v': nrm((DEPTH, CMP_LEN, HEAD_DIM, CMP_HID), (CMP_LEN * HEAD_DIM) ** -0.5),
        'cmp_w2_v': nrm((DEPTH, CMP_HID, HEAD_DIM), CMP_HID ** -0.5),
        'conv_w': nrm((DEPTH, CONV_W, C_CONV), CONV_W ** -0.5),
        'conv_b': nrm((DEPTH, C_CONV), 0.02),
        'conv_ln_g': 1.0 + nrm((DEPTH, C_CONV), 0.02),
        'conv_ln_b': nrm((DEPTH, C_CONV), 0.02),
        'w_conv_out': nrm((DEPTH, C_CONV, D_MODEL), C_CONV ** -0.5),
        'w_attn_out': nrm((DEPTH, N_HEADS * HEAD_DIM, D_MODEL), (N_HEADS * HEAD_DIM) ** -0.5),
        'w_o': nrm((DEPTH, D_MODEL, D_MODEL), BETA * D_MODEL ** -0.5),
        'ln1_g': 1.0 + nrm((DEPTH, D_MODEL), 0.02),
        'ln1_b': nrm((DEPTH, D_MODEL), 0.02),
        'w_gate': nrm((DEPTH, D_MODEL, D_FF), D_MODEL ** -0.5),
        'w_up': nrm((DEPTH, D_MODEL, D_FF), D_MODEL ** -0.5),
        'w_down': nrm((DEPTH, D_FF, D_MODEL), BETA * D_FF ** -0.5),
        'ln2_g': 1.0 + nrm((DEPTH, D_MODEL), 0.02),
        'ln2_b': nrm((DEPTH, D_MODEL), 0.02),
    }


def reference(x_prompt, x_sample, cache_cmp_k, cache_cmp_v, cache_slc_k, cache_slc_v,
              state_win_k, state_win_v, state_conv, page_table,
              w_in, cmp_pe_k, cmp_w1_k, cmp_w2_k, cmp_pe_v, cmp_w1_v, cmp_w2_v,
              conv_w, conv_b, conv_ln_g, conv_ln_b, w_conv_out, w_attn_out, w_o,
              ln1_g, ln1_b, w_gate, w_up, w_down, ln2_g, ln2_b):
    past_len = page_table.shape[1] * cache_cmp_k.shape[2]
    xp, xs = x_prompt, x_sample
    new_p, new_s = [], []
    for l in range(DEPTH):
        mix = (w_in[l], cmp_pe_k[l], cmp_w1_k[l], cmp_w2_k[l], cmp_pe_v[l], cmp_w1_v[l], cmp_w2_v[l],
               conv_w[l], conv_b[l], conv_ln_g[l], conv_ln_b[l], w_conv_out[l], w_attn_out[l], w_o[l])
        ffn = (ln1_g[l], ln1_b[l], w_gate[l], w_up[l], w_down[l], ln2_g[l], ln2_b[l])
        past = (state_conv[l],
                gather_pages(cache_cmp_k[l], page_table), gather_pages(cache_cmp_v[l], page_table),
                gather_pages(cache_slc_k[l], page_table), gather_pages(cache_slc_v[l], page_table),
                state_win_k[l], state_win_v[l])
        xp, st_p = decoder_layer(xp, 0, None, mix, ffn)
        xs, st_s = decoder_layer(xs, past_len, past, mix, ffn)
        new_p.append(st_p)
        new_s.append(st_s)
    stk = lambda rows, i: jnp.stack([r[i] for r in rows], axis=0)
    return (xp, xs,
            stk(new_p, 1), stk(new_p, 2), stk(new_p, 3), stk(new_p, 4), stk(new_p, 5), stk(new_p, 6), stk(new_p, 0),
            stk(new_s, 1), stk(new_s, 2), stk(new_s, 3), stk(new_s, 4), stk(new_s, 5), stk(new_s, 6), stk(new_s, 0))
```

```python
import functools
import math

import numpy as np
import jax
import jax.numpy as jnp
from jax import lax
from jax.experimental import pallas as pl
from jax.experimental.pallas import tpu as pltpu

F32 = jnp.float32
BF16 = jnp.bfloat16

C_CONV = 512
CONV_W = 31
N_HEADS = 8
N_KV = 2
HG = N_HEADS // N_KV
HEAD_DIM = 64
CMP_LEN = 32
CMP_STRIDE = 16
CMP_HID = 64
SLC_BLK = 64
N_SEL = 16
WINDOW = 512
ROPE_THETA = 10000.0
LN_EPS = 1e-5
MASK_VALUE = -1e30
FORCE_SCORE = 1e4
KV_W = N_KV * HEAD_DIM
QW = N_HEADS * HEAD_DIM
SCALE = HEAD_DIM ** -0.5

LANES = 128
SUBLANES = 8
VMEM_LIMIT = 56 * 1024 * 1024

OFF_UA = 0
OFF_UG = C_CONV
OFF_Q = 2 * C_CONV
OFF_KC = OFF_Q + QW
OFF_VC = OFF_KC + KV_W
OFF_KS = OFF_VC + KV_W
OFF_VS = OFF_KS + KV_W
OFF_KW = OFF_VS + KV_W
OFF_VW = OFF_KW + KV_W
OFF_GN = OFF_VW + KV_W
OFF_GA = OFF_GN + 3 * N_HEADS
ZW = OFF_GN + LANES

ROW_TILE = 512
MERGE_TILE = 256
DEC_PAD = 8
ATT_TQ = 128
SLC_TK = 256
CONV_TT = 64
SEL_ROWS_PROMPT = 128
SEL_ROWS_SAMPLE = 256


def _params(*sem):
    return pltpu.CompilerParams(dimension_semantics=sem, vmem_limit_bytes=VMEM_LIMIT)


def _sigmoid(x):
    return 1.0 / (1.0 + jnp.exp(-x))


def _silu(x):
    return x * _sigmoid(x)


def _gelu_tanh(x):
    return 0.5 * x * (1.0 + jnp.tanh(math.sqrt(2.0 / math.pi) * (x + 0.044715 * (x * x * x))))


def _layer_norm(y, g, b):
    mu = jnp.mean(y, axis=-1, keepdims=True)
    d = y - mu
    var = jnp.mean(d * d, axis=-1, keepdims=True)
    return d * lax.rsqrt(var + LN_EPS) * g + b


def _dot(a, b):
    return jnp.dot(a, b, preferred_element_type=F32)


def _dot_nt(a, b):
    return lax.dot_general(a, b, (((1,), (1,)), ((), ())), preferred_element_type=F32)


def _split_bf16(a, terms):
    parts, rest = [], a
    for _ in range(terms):
        part = rest.astype(BF16)
        parts.append(part)
        rest = rest - part.astype(F32)
    return parts


def _rope_tile(x, cos, sin_signed):
    lane = lax.broadcasted_iota(jnp.int32, x.shape, 1)
    first = (lane % HEAD_DIM) < (HEAD_DIM // 2)
    rot = jnp.where(first, pltpu.roll(x, LANES - HEAD_DIM // 2, 1), pltpu.roll(x, HEAD_DIM // 2, 1))
    return x * cos + rot * sin_signed


def _proj_kernel(x_ref, w_ref, cos_ref, sin_ref,
                 u_ref, q_ref, qr_ref, kc_ref, vc_ref, ks_ref, vs_ref, kw_ref, vw_ref, g_ref):
    xb = x_ref[...].astype(BF16)

    def seg(off, n):
        return _dot(xb, w_ref[:, off:off + n])

    cos = cos_ref[...]
    sin = sin_ref[...]
    u_ref[...] = seg(OFF_UA, C_CONV) * _sigmoid(seg(OFF_UG, C_CONV))
    for c in range(QW // LANES):
        qc = seg(OFF_Q + c * LANES, LANES)
        q_ref[:, c * LANES:(c + 1) * LANES] = qc
        qr_ref[:, c * LANES:(c + 1) * LANES] = _rope_tile(qc, cos, sin)
    kc_ref[...] = seg(OFF_KC, KV_W)
    vc_ref[...] = seg(OFF_VC, KV_W)
    ks_ref[...] = _rope_tile(seg(OFF_KS, KV_W), cos, sin)
    vs_ref[...] = seg(OFF_VS, KV_W)
    kw_ref[...] = _rope_tile(seg(OFF_KW, KV_W), cos, sin)
    vw_ref[...] = seg(OFF_VW, KV_W)
    g_ref[...] = _sigmoid(seg(OFF_GN, LANES))


def _proj(x, w, cos_t, sin_t, n_prompt_tiles, tiles_per_seq):
    rows, d = x.shape
    tm = ROW_TILE

    def tab_map(i):
        return (jnp.where(i < n_prompt_tiles, i % tiles_per_seq, tiles_per_seq), 0)

    row = lambda n: pl.BlockSpec((tm, n), lambda i: (i, 0))
    widths = (C_CONV, QW, QW) + (KV_W,) * 6 + (LANES,)
    return pl.pallas_call(
        _proj_kernel,
        grid=(rows // tm,),
        in_specs=[row(d), pl.BlockSpec((d, ZW), lambda i: (0, 0)),
                  pl.BlockSpec((tm, LANES), tab_map), pl.BlockSpec((tm, LANES), tab_map)],
        out_specs=[row(n) for n in widths],
        out_shape=[jax.ShapeDtypeStruct((rows, n), F32) for n in widths],
        compiler_params=_params("parallel"),
        name="proj_in",
    )(x, w, cos_t, sin_t)


def _conv_prompt_kernel(u_ref, w_ref, b_ref, g_ref, beta_ref, c_ref, ext_ref):
    t_len = u_ref.shape[0]
    hist = ext_ref.shape[0] - t_len
    ext_ref[0:hist, :] = jnp.zeros((hist, C_CONV), F32)
    ext_ref[hist:, :] = u_ref[...]
    first = hist - (CONV_W - 1)

    def body(i, carry):
        t0 = pl.multiple_of(i * CONV_TT, CONV_TT)
        win = ext_ref[pl.ds(t0, CONV_TT + hist), :]
        acc = jnp.zeros((CONV_TT, C_CONV), F32) + b_ref[...]
        for k in range(CONV_W):
            acc = acc + win[first + k:first + k + CONV_TT] * w_ref[k:k + 1, :]
        c_ref[pl.ds(t0, CONV_TT), :] = _silu(_layer_norm(acc, g_ref[...], beta_ref[...]))
        return carry

    lax.fori_loop(0, t_len // CONV_TT, body, 0)


def _conv_prompt(u, w, b, g, beta, n_seq, t_len):
    hist = 32
    vec = pl.BlockSpec((1, C_CONV), lambda i: (0, 0))
    return pl.pallas_call(
        _conv_prompt_kernel,
        grid=(n_seq,),
        in_specs=[pl.BlockSpec((t_len, C_CONV), lambda i: (i, 0)),
                  pl.BlockSpec((CONV_W, C_CONV), lambda i: (0, 0)), vec, vec, vec],
        out_specs=pl.BlockSpec((t_len, C_CONV), lambda i: (i, 0)),
        out_shape=jax.ShapeDtypeStruct((n_seq * t_len, C_CONV), F32),
        scratch_shapes=[pltpu.VMEM((hist + t_len, C_CONV), F32)],
        compiler_params=_params("parallel"),
        name="conv_prompt",
    )(u, w, b, g, beta)


def _conv_sample_kernel(st_ref, u_ref, w_ref, b_ref, g_ref, beta_ref, c_ref):
    n_prev = st_ref.shape[0]
    n_new = u_ref.shape[0]
    for t in range(n_new):
        acc = jnp.zeros(c_ref.shape[1:], F32) + b_ref[...]
        for k in range(CONV_W):
            j = t + k
            row = st_ref[j] if j < n_prev else u_ref[j - n_prev]
            acc = acc + row * w_ref[k:k + 1, :]
        c_ref[t] = _silu(_layer_norm(acc, g_ref[...], beta_ref[...]))


def _conv_sample(state_t, u_t, w, b, g, beta):
    return pl.pallas_call(
        _conv_sample_kernel,
        out_shape=jax.ShapeDtypeStruct(u_t.shape, F32),
        compiler_params=pltpu.CompilerParams(vmem_limit_bytes=VMEM_LIMIT),
        name="conv_sample",
    )(state_t, u_t, w, b, g, beta)


def _compress_core(ch, pe_ref, w1_ref, w2_ref):
    n = ch.shape[0]
    hcat = _dot(ch.astype(BF16), w1_ref[...])
    pcat = sum(_dot(part, w1_ref[...]) for part in _split_bf16(pe_ref[...], 2))
    h0 = hcat[:, :LANES] + pcat[0:1, :LANES]
    h1 = hcat[:, LANES:] + pcat[1:2, LANES:]
    h = h0 + pltpu.roll(h1, n - 1, 0)
    return _dot(_gelu_tanh(h).astype(BF16), w2_ref[...])


def _compress_prompt_kernel(ch_ref, pe_ref, w1_ref, w2_ref, o_ref):
    o_ref[0] = _compress_core(ch_ref[...], pe_ref, w1_ref, w2_ref)


def _compress_prompt(rows, pe2, w1b, w2b, n_seq, t_len):
    n_ch = t_len // CMP_STRIDE
    kdim = CMP_STRIDE * KV_W
    ch = rows.reshape(rows.shape[0] // CMP_STRIDE, kdim)
    return pl.pallas_call(
        _compress_prompt_kernel,
        grid=(n_seq,),
        in_specs=[pl.BlockSpec((n_ch, kdim), lambda i: (i, 0)),
                  pl.BlockSpec((SUBLANES, kdim), lambda i: (0, 0)),
                  pl.BlockSpec((kdim, 2 * LANES), lambda i: (0, 0)),
                  pl.BlockSpec((LANES, LANES), lambda i: (0, 0))],
        out_specs=pl.BlockSpec((1, n_ch, LANES), lambda i: (i, 0, 0)),
        out_shape=jax.ShapeDtypeStruct((n_seq, n_ch, LANES), F32),
        compiler_params=_params("parallel"),
        name="compress_prompt",
    )(ch, pe2, w1b, w2b)


def _page_copy(cache_ref, buf_ref, sem_ref, page, slot, p):
    return pltpu.make_async_copy(cache_ref.at[page], buf_ref.at[slot, p], sem_ref.at[slot])


def _start_pages(pt_ref, cache_ref, buf_ref, sem_ref, seq, slot, n_pages):
    def body(p, carry):
        _page_copy(cache_ref, buf_ref, sem_ref, pt_ref[seq, p], slot, p).start()
        return carry
    lax.fori_loop(0, n_pages, body, 0)


def _wait_pages(cache_ref, buf_ref, sem_ref, slot, n_pages):
    def body(p, carry):
        _page_copy(cache_ref, buf_ref, sem_ref, 0, slot, p).wait()
        return carry
    lax.fori_loop(0, n_pages, body, 0)


def _compress_sample_kernel(pt_ref, cache_ref, pe_ref, w1_ref, w2_ref, o_ref, buf_ref, sem_ref):
    b = pl.program_id(0)
    nb = pl.num_programs(0)
    n_pages = buf_ref.shape[1]
    slot = b % 2

    @pl.when(b == 0)
    def _():
        _start_pages(pt_ref, cache_ref, buf_ref, sem_ref, 0, 0, n_pages)

    @pl.when(b + 1 < nb)
    def _():
        _start_pages(pt_ref, cache_ref, buf_ref, sem_ref, b + 1, 1 - slot, n_pages)

    _wait_pages(cache_ref, buf_ref, sem_ref, slot, n_pages)
    ch = buf_ref[slot].reshape(n_pages * buf_ref.shape[2], buf_ref.shape[3])
    o_ref[0] = _compress_core(ch, pe_ref, w1_ref, w2_ref)


def _compress_sample(page_table, cache, pe2, w1b, w2b):
    n_pool, page = cache.shape[0], cache.shape[1]
    kdim = CMP_STRIDE * KV_W
    cpp = page // CMP_STRIDE
    cache_c = cache.reshape(n_pool, cpp, kdim)
    n_seq, n_pages = page_table.shape
    n_ch = n_pages * cpp
    grid_spec = pltpu.PrefetchScalarGridSpec(
        num_scalar_prefetch=1,
        grid=(n_seq,),
        in_specs=[pl.BlockSpec(memory_space=pl.ANY),
                  pl.BlockSpec((SUBLANES, kdim), lambda i, pt: (0, 0)),
                  pl.BlockSpec((kdim, 2 * LANES), lambda i, pt: (0, 0)),
                  pl.BlockSpec((LANES, LANES), lambda i, pt: (0, 0))],
        out_specs=pl.BlockSpec((1, n_ch, LANES), lambda i, pt: (i, 0, 0)),
        scratch_shapes=[pltpu.VMEM((2, n_pages, cpp, kdim), F32), pltpu.SemaphoreType.DMA((2,))],
    )
    return pl.pallas_call(
        _compress_sample_kernel,
        grid_spec=grid_spec,
        out_shape=jax.ShapeDtypeStruct((n_seq, n_ch, LANES), F32),
        compiler_params=_params("arbitrary"),
        name="compress_sample",
    )(page_table, cache_c, pe2, w1b, w2b)


def _select_mask(score_t, qpos_row, n_slc):
    p_rows = score_t.shape[0]
    j = lax.broadcasted_iota(jnp.int32, score_t.shape, 0)
    cur = qpos_row // SLC_BLK
    forced = (j == 0) | (j == cur) | (j == cur - 1)
    sc = jnp.where(forced, FORCE_SCORE, jnp.where(j <= cur, score_t, -1.0))
    sc = jnp.where(j < n_slc, sc, -2.0)
    sel = jnp.zeros(score_t.shape, F32)
    for _ in range(min(N_SEL, n_slc)):
        mx = jnp.max(sc, axis=0, keepdims=True)
        idx = jnp.min(jnp.where(sc == mx, j, p_rows), axis=0, keepdims=True)
        hit = j == idx
        sel = jnp.where(hit, 1.0, sel)
        sc = jnp.where(hit, -3.0, sc)
    return sel


def _cmp_probs(qh, kg, mask, any_row):
    s = _dot_nt(qh.astype(BF16), kg) * SCALE
    s = jnp.where(mask, s, MASK_VALUE)
    e = jnp.exp(s - jnp.max(s, axis=-1, keepdims=True))
    return e / jnp.sum(e, axis=-1, keepdims=True) * any_row


def _cmp_sel_prompt_kernel(q_ref, kc_ref, vc_ref, g_ref, mt_ref, o_ref, sel_ref, *, n_cmp, n_slc):
    tq = q_ref.shape[0]
    n_pad = kc_ref.shape[1]
    q0 = pl.program_id(1) * tq
    qpos_col = q0 + lax.broadcasted_iota(jnp.int32, (tq, 1), 0)
    qpos_row = q0 + lax.broadcasted_iota(jnp.int32, (1, tq), 1)
    n_idx = lax.broadcasted_iota(jnp.int32, (tq, n_pad), 1)
    mask = (n_idx * CMP_STRIDE + (CMP_LEN - 1) <= qpos_col) & (n_idx < n_cmp)
    any_row = (qpos_col >= CMP_LEN - 1).astype(F32)
    gates = g_ref[...]
    for g in range(N_KV):
        kg = kc_ref[0, :, g * HEAD_DIM:(g + 1) * HEAD_DIM].astype(BF16)
        vg = vc_ref[0, :, g * HEAD_DIM:(g + 1) * HEAD_DIM].astype(BF16)
        psum = jnp.zeros((tq, n_pad), F32)
        for h in range(HG):
            hh = g * HG + h
            p = _cmp_probs(q_ref[:, hh * HEAD_DIM:(hh + 1) * HEAD_DIM], kg, mask, any_row)
            psum = psum + p
            o_ref[:, hh * HEAD_DIM:(hh + 1) * HEAD_DIM] = _dot(p.astype(BF16), vg) * gates[:, 3 * hh:3 * hh + 1]
        score_t = sum(_dot_nt(mt_ref[...], part) for part in _split_bf16(psum, 3))
        sel_ref[g] = _select_mask(score_t, qpos_row, n_slc).T


def _cmp_sel_prompt(q, kcmp, vcmp, gates, mt, n_seq, t_len, n_cmp, n_slc):
    tq = ATT_TQ
    nq = t_len // tq
    n_pad = kcmp.shape[1]
    rows = n_seq * t_len
    kv = pl.BlockSpec((1, n_pad, LANES), lambda b, i: (b, 0, 0))
    return pl.pallas_call(
        functools.partial(_cmp_sel_prompt_kernel, n_cmp=n_cmp, n_slc=n_slc),
        grid=(n_seq, nq),
        in_specs=[pl.BlockSpec((tq, QW), lambda b, i: (b * nq + i, 0)), kv, kv,
                  pl.BlockSpec((tq, LANES), lambda b, i: (b * nq + i, 0)),
                  pl.BlockSpec(mt.shape, lambda b, i: (0, 0))],
        out_specs=[pl.BlockSpec((tq, QW), lambda b, i: (b * nq + i, 0)),
                   pl.BlockSpec((N_KV, tq, SEL_ROWS_PROMPT), lambda b, i: (0, b * nq + i, 0))],
        out_shape=[jax.ShapeDtypeStruct((rows, QW), F32),
                   jax.ShapeDtypeStruct((N_KV, rows, SEL_ROWS_PROMPT), F32)],
        compiler_params=_params("parallel", "parallel"),
        name="cmp_select_prompt",
    )(q, kcmp, vcmp, gates, mt)


def _slc_prompt_kernel(qr_ref, ks_ref, vs_ref, sel_ref, e_ref, g_ref, o_ref, qb_ref, m_ref, l_ref, acc_ref):
    tq = qr_ref.shape[0]
    tk = e_ref.shape[2]
    q0 = pl.program_id(1) * tq
    qpos = q0 + lax.broadcasted_iota(jnp.int32, (tq, 1), 0)
    qb_ref[...] = qr_ref[...].astype(BF16)
    m_ref[...] = jnp.full(m_ref.shape, MASK_VALUE, F32)
    l_ref[...] = jnp.zeros(l_ref.shape, F32)
    acc_ref[...] = jnp.zeros(acc_ref.shape, F32)

    def body(c, carry):
        k0 = pl.multiple_of(c * tk, tk)
        causal = (k0 + lax.broadcasted_iota(jnp.int32, (tq, tk), 1)) <= qpos
        for g in range(N_KV):
            chosen = _dot(sel_ref[g].astype(BF16), e_ref[c])
            mask = causal & (chosen > 0.5)
            kg = ks_ref[pl.ds(k0, tk), g * HEAD_DIM:(g + 1) * HEAD_DIM].astype(BF16)
            vg = vs_ref[pl.ds(k0, tk), g * HEAD_DIM:(g + 1) * HEAD_DIM].astype(BF16)
            for h in range(HG):
                hh = g * HG + h
                s = _dot_nt(qb_ref[:, hh * HEAD_DIM:(hh + 1) * HEAD_DIM], kg) * SCALE
                s = jnp.where(mask, s, MASK_VALUE)
                m_old = m_ref[hh]
                m_new = jnp.maximum(m_old, jnp.max(s, axis=-1, keepdims=True))
                p = jnp.where(mask, jnp.exp(s - m_new), 0.0)
                alpha = jnp.exp(m_old - m_new)
                l_ref[hh] = alpha * l_ref[hh] + jnp.sum(p, axis=-1, keepdims=True)
                acc_ref[hh] = alpha * acc_ref[hh] + _dot(p.astype(BF16), vg)
                m_ref[hh] = m_new
        return carry

    lax.fori_loop(0, (q0 + tq + tk - 1) // tk, body, 0)
    gates = g_ref[...]
    for hh in range(N_HEADS):
        o_ref[:, hh * HEAD_DIM:(hh + 1) * HEAD_DIM] = acc_ref[hh] / l_ref[hh] * gates[:, 3 * hh + 1:3 * hh + 2]


def _slc_prompt(qr, ks, vs, sel, expand, gates, n_seq, t_len):
    tq = ATT_TQ
    nq = t_len // tq
    rows = n_seq * t_len
    kv = pl.BlockSpec((t_len, KV_W), lambda b, i: (b, 0))
    return pl.pallas_call(
        _slc_prompt_kernel,
        grid=(n_seq, nq),
        in_specs=[pl.BlockSpec((tq, QW), lambda b, i: (b * nq + i, 0)), kv, kv,
                  pl.BlockSpec((N_KV, tq, SEL_ROWS_PROMPT), lambda b, i: (0, b * nq + i, 0)),
                  pl.BlockSpec(expand.shape, lambda b, i: (0, 0, 0)),
                  pl.BlockSpec((tq, LANES), lambda b, i: (b * nq + i, 0))],
        out_specs=pl.BlockSpec((tq, QW), lambda b, i: (b * nq + i, 0)),
        out_shape=jax.ShapeDtypeStruct((rows, QW), F32),
        scratch_shapes=[pltpu.VMEM((tq, QW), BF16),
                        pltpu.VMEM((N_HEADS, tq, 1), F32), pltpu.VMEM((N_HEADS, tq, 1), F32),
                        pltpu.VMEM((N_HEADS, tq, HEAD_DIM), F32)],
        compiler_params=_params("parallel", "parallel"),
        name="slc_prompt",
    )(qr, ks, vs, sel, expand, gates)


def _win_prompt_kernel(qr_ref, kw_ref, vw_ref, g_ref, o_ref, *, nk):
    tq = qr_ref.shape[0]
    q0 = pl.program_id(1) * tq
    start = pl.multiple_of(jnp.maximum(q0 + tq - nk, 0), tq)
    qpos = q0 + lax.broadcasted_iota(jnp.int32, (tq, 1), 0)
    diff = qpos - (start + lax.broadcasted_iota(jnp.int32, (tq, nk), 1))
    mask = (diff >= 0) & (diff < WINDOW)
    gates = g_ref[...]
    for g in range(N_KV):
        kg = kw_ref[pl.ds(start, nk), g * HEAD_DIM:(g + 1) * HEAD_DIM].astype(BF16)
        vg = vw_ref[pl.ds(start, nk), g * HEAD_DIM:(g + 1) * HEAD_DIM].astype(BF16)
        for h in range(HG):
            hh = g * HG + h
            s = _dot_nt(qr_ref[:, hh * HEAD_DIM:(hh + 1) * HEAD_DIM].astype(BF16), kg) * SCALE
            s = jnp.where(mask, s, MASK_VALUE)
            e = jnp.exp(s - jnp.max(s, axis=-1, keepdims=True))
            p = e / jnp.sum(e, axis=-1, keepdims=True)
            o_ref[:, hh * HEAD_DIM:(hh + 1) * HEAD_DIM] = _dot(p.astype(BF16), vg) * gates[:, 3 * hh + 2:3 * hh + 3]


def _win_prompt(qr, kw, vw, gates, n_seq, t_len):
    tq = ATT_TQ
    nq = t_len // tq
    nk = min(WINDOW + tq, t_len)
    kv = pl.BlockSpec((t_len, KV_W), lambda b, i: (b, 0))
    return pl.pallas_call(
        functools.partial(_win_prompt_kernel, nk=nk),
        grid=(n_seq, nq),
        in_specs=[pl.BlockSpec((tq, QW), lambda b, i: (b * nq + i, 0)), kv, kv,
                  pl.BlockSpec((tq, LANES), lambda b, i: (b * nq + i, 0))],
        out_specs=pl.BlockSpec((tq, QW), lambda b, i: (b * nq + i, 0)),
        out_shape=jax.ShapeDtypeStruct((n_seq * t_len, QW), F32),
        compiler_params=_params("parallel", "parallel"),
        name="win_prompt",
    )(qr, kw, vw, gates)


def _cmp_win_sample_kernel(q_ref, qr_ref, kc_ref, vc_ref, wk_ref, wv_ref, nk_ref, nv_ref, g_ref, mt_ref,
                           o_ref, sel_ref, *, past, n_cmp, n_slc, n_new):
    nq = q_ref.shape[1]
    n_pad = kc_ref.shape[1]
    wbuf = wk_ref.shape[1]
    qpos_col = past + lax.broadcasted_iota(jnp.int32, (nq, 1), 0)
    qpos_row = past + lax.broadcasted_iota(jnp.int32, (1, LANES), 1)
    n_idx = lax.broadcasted_iota(jnp.int32, (nq, n_pad), 1)
    mask_c = (n_idx * CMP_STRIDE + (CMP_LEN - 1) <= qpos_col) & (n_idx < n_cmp)
    any_row = (qpos_col >= CMP_LEN - 1).astype(F32)
    t_col = lax.broadcasted_iota(jnp.int32, (nq, 1), 0)
    diff_buf = t_col + wbuf - lax.broadcasted_iota(jnp.int32, (nq, wbuf), 1)
    mask_buf = (diff_buf >= 0) & (diff_buf < WINDOW)
    r_idx = lax.broadcasted_iota(jnp.int32, (nq, nq), 1)
    mask_new = (t_col - r_idx >= 0) & (t_col - r_idx < WINDOW) & (r_idx < n_new)
    gates = g_ref[0]
    for g in range(N_KV):
        lanes = slice(g * HEAD_DIM, (g + 1) * HEAD_DIM)
        kg = kc_ref[0, :, lanes].astype(BF16)
        vg = vc_ref[0, :, lanes].astype(BF16)
        wkg = wk_ref[0, :, lanes].astype(BF16)
        wvg = wv_ref[0, :, lanes].astype(BF16)
        nkg = nk_ref[0, :, lanes].astype(BF16)
        nvg = nv_ref[0, :, lanes].astype(BF16)
        psum = jnp.zeros((nq, n_pad), F32)
        for h in range(HG):
            hh = g * HG + h
            hl = slice(hh * HEAD_DIM, (hh + 1) * HEAD_DIM)
            p = _cmp_probs(q_ref[0, :, hl], kg, mask_c, any_row)
            psum = psum + p
            o_cmp = _dot(p.astype(BF16), vg)
            qh = qr_ref[0, :, hl].astype(BF16)
            s1 = jnp.where(mask_buf, _dot_nt(qh, wkg) * SCALE, MASK_VALUE)
            s2 = jnp.where(mask_new, _dot_nt(qh, nkg) * SCALE, MASK_VALUE)
            m = jnp.maximum(jnp.max(s1, axis=-1, keepdims=True), jnp.max(s2, axis=-1, keepdims=True))
            e1 = jnp.exp(s1 - m)
            e2 = jnp.exp(s2 - m)
            den = jnp.sum(e1, axis=-1, keepdims=True) + jnp.sum(e2, axis=-1, keepdims=True)
            o_win = _dot((e1 / den).astype(BF16), wvg) + _dot((e2 / den).astype(BF16), nvg)
            o_ref[0, :, hl] = o_cmp * gates[:, 3 * hh:3 * hh + 1] + o_win * gates[:, 3 * hh + 2:3 * hh + 3]
        psum_pad = jnp.concatenate([psum, jnp.zeros((LANES - nq, n_pad), F32)], axis=0)
        score_t = sum(_dot_nt(mt_ref[...], part) for part in _split_bf16(psum_pad, 3))
        sel = _select_mask(score_t, qpos_row, n_slc).T
        sel_ref[0, g] = sel[0:nq, :]


def _cmp_win_sample(q, qr, kcmp, vcmp, wk, wv, nk, nv, gates, mt, past, n_cmp, n_slc, n_new):
    n_seq, nq, _ = q.shape
    per_seq = lambda a: pl.BlockSpec((1,) + a.shape[1:], lambda b: (b, 0, 0))
    return pl.pallas_call(
        functools.partial(_cmp_win_sample_kernel, past=past, n_cmp=n_cmp, n_slc=n_slc, n_new=n_new),
        grid=(n_seq,),
        in_specs=[per_seq(a) for a in (q, qr, kcmp, vcmp, wk, wv, nk, nv, gates)]
                 + [pl.BlockSpec(mt.shape, lambda b: (0, 0))],
        out_specs=[pl.BlockSpec((1, nq, QW), lambda b: (b, 0, 0)),
                   pl.BlockSpec((1, N_KV, nq, SEL_ROWS_SAMPLE), lambda b: (b, 0, 0, 0))],
        out_shape=[jax.ShapeDtypeStruct((n_seq, nq, QW), F32),
                   jax.ShapeDtypeStruct((n_seq, N_KV, nq, SEL_ROWS_SAMPLE), F32)],
        compiler_params=_params("parallel"),
        name="cmp_win_sample",
    )(q, qr, kcmp, vcmp, wk, wv, nk, nv, gates, mt)


def _slc_sample_kernel(pt_ref, qr_ref, sel_ref, nk_ref, nv_ref, g_ref, part_ref, e_ref, kcache_ref, vcache_ref,
                       o_ref, kbuf_ref, vbuf_ref, ksem_ref, vsem_ref, *, n_new):
    b = pl.program_id(0)
    nb = pl.num_programs(0)
    n_pages = kbuf_ref.shape[1]
    page = kbuf_ref.shape[2]
    nq = qr_ref.shape[1]
    slot = b % 2

    def start(seq, sl):
        _start_pages(pt_ref, kcache_ref, kbuf_ref, ksem_ref, seq, sl, n_pages)
        _start_pages(pt_ref, vcache_ref, vbuf_ref, vsem_ref, seq, sl, n_pages)

    @pl.when(b == 0)
    def _():
        start(0, 0)

    @pl.when(b + 1 < nb)
    def _():
        start(b + 1, 1 - slot)

    _wait_pages(kcache_ref, kbuf_ref, ksem_ref, slot, n_pages)
    _wait_pages(vcache_ref, vbuf_ref, vsem_ref, slot, n_pages)
    k_all = kbuf_ref[slot].reshape(n_pages * page, KV_W)
    v_all = vbuf_ref[slot].reshape(n_pages * page, KV_W)
    t_col = lax.broadcasted_iota(jnp.int32, (nq, 1), 0)
    r_idx = lax.broadcasted_iota(jnp.int32, (nq, nq), 1)
    mask_new = (r_idx <= t_col) & (r_idx < n_new)
    gates = g_ref[0]
    for g in range(N_KV):
        lanes = slice(g * HEAD_DIM, (g + 1) * HEAD_DIM)
        chosen = _dot(sel_ref[0, g].astype(BF16), e_ref[...]) > 0.5
        kg = k_all[:, lanes].astype(BF16)
        vg = v_all[:, lanes].astype(BF16)
        nkg = nk_ref[0, :, lanes].astype(BF16)
        nvg = nv_ref[0, :, lanes].astype(BF16)
        for h in range(HG):
            hh = g * HG + h
            hl = slice(hh * HEAD_DIM, (hh + 1) * HEAD_DIM)
            qh = qr_ref[0, :, hl].astype(BF16)
            s1 = jnp.where(chosen, _dot_nt(qh, kg) * SCALE, MASK_VALUE)
            s2 = jnp.where(mask_new, _dot_nt(qh, nkg) * SCALE, MASK_VALUE)
            m = jnp.maximum(jnp.max(s1, axis=-1, keepdims=True), jnp.max(s2, axis=-1, keepdims=True))
            e1 = jnp.where(chosen, jnp.exp(s1 - m), 0.0)
            e2 = jnp.where(mask_new, jnp.exp(s2 - m), 0.0)
            den = jnp.sum(e1, axis=-1, keepdims=True) + jnp.sum(e2, axis=-1, keepdims=True)
            o_slc = _dot((e1 / den).astype(BF16), vg) + _dot((e2 / den).astype(BF16), nvg)
            o_ref[0, :, hl] = part_ref[0, :, hl] + o_slc * gates[:, 3 * hh + 1:3 * hh + 2]


def _slc_sample(page_table, qr, sel, nk, nv, gates, part, expand, kcache, vcache, n_new):
    n_seq, nq, _ = qr.shape
    n_pages = page_table.shape[1]
    n_pool, page = kcache.shape[0], kcache.shape[1]
    kc2 = kcache.reshape(n_pool, page, KV_W)
    vc2 = vcache.reshape(n_pool, page, KV_W)
    per_seq = lambda a: pl.BlockSpec((1,) + a.shape[1:], lambda b, pt: (b,) + (0,) * (a.ndim - 1))
    grid_spec = pltpu.PrefetchScalarGridSpec(
        num_scalar_prefetch=1,
        grid=(n_seq,),
        in_specs=[per_seq(a) for a in (qr, sel, nk, nv, gates, part)]
                 + [pl.BlockSpec(expand.shape, lambda b, pt: (0, 0)),
                    pl.BlockSpec(memory_space=pl.ANY), pl.BlockSpec(memory_space=pl.ANY)],
        out_specs=pl.BlockSpec((1, nq, QW), lambda b, pt: (b, 0, 0)),
        scratch_shapes=[pltpu.VMEM((2, n_pages, page, KV_W), F32), pltpu.VMEM((2, n_pages, page, KV_W), F32),
                        pltpu.SemaphoreType.DMA((2,)), pltpu.SemaphoreType.DMA((2,))],
    )
    return pl.pallas_call(
        functools.partial(_slc_sample_kernel, n_new=n_new),
        grid_spec=grid_spec,
        out_shape=jax.ShapeDtypeStruct((n_seq, nq, QW), F32),
        compiler_params=_params("arbitrary"),
        name="slc_sample",
    )(page_table, qr, sel, nk, nv, gates, part, expand, kc2, vc2)


def _merge_kernel(x_ref, cp_ref, cs_ref, op1_ref, op2_ref, op3_ref, os_ref,
                  wga_ref, wgb_ref, wco_ref, wao_ref, wo_ref, g_ref, b_ref, y_ref, *, n_prompt_tiles, alpha):
    is_prompt = pl.program_id(0) < n_prompt_tiles
    x = x_ref[...]
    xb = x.astype(BF16)
    c = jnp.where(is_prompt, cp_ref[...], cs_ref[...])
    o = jnp.where(is_prompt, op1_ref[...] + op2_ref[...] + op3_ref[...], os_ref[...])
    mixed = _sigmoid(_dot(xb, wga_ref[...])) * _dot(c.astype(BF16), wco_ref[...])
    mixed = mixed + _sigmoid(_dot(xb, wgb_ref[...])) * _dot(o.astype(BF16), wao_ref[...])
    y = _dot(mixed.astype(BF16), wo_ref[...])
    y_ref[...] = _layer_norm(alpha * x + y, g_ref[...], b_ref[...])


def _merge(x, c_p, c_s, o_cmp, o_slc, o_win, o_s, wga, wgb, wco, wao, wo, g, b, alpha):
    rows, d = x.shape
    tm = MERGE_TILE
    n_pt = c_p.shape[0] // tm
    prm = lambda n: pl.BlockSpec((tm, n), lambda i: (jnp.minimum(i, n_pt - 1), 0))
    smp = lambda n: pl.BlockSpec((tm, n), lambda i: (jnp.maximum(i - n_pt, 0), 0))
    full = lambda a: pl.BlockSpec(a.shape, lambda i: (0, 0))
    return pl.pallas_call(
        functools.partial(_merge_kernel, n_prompt_tiles=n_pt, alpha=alpha),
        grid=(rows // tm,),
        in_specs=[pl.BlockSpec((tm, d), lambda i: (i, 0)), prm(C_CONV), smp(C_CONV),
                  prm(QW), prm(QW), prm(QW), smp(QW),
                  full(wga), full(wgb), full(wco), full(wao), full(wo), full(g), full(b)],
        out_specs=pl.BlockSpec((tm, d), lambda i: (i, 0)),
        out_shape=jax.ShapeDtypeStruct((rows, d), F32),
        compiler_params=_params("parallel"),
        name="merge",
    )(x, c_p, c_s, o_cmp, o_slc, o_win, o_s, wga, wgb, wco, wao, wo, g, b)


def _ffn_kernel(x_ref, wg_ref, wu_ref, wd_ref, g_ref, b_ref, y_ref, xb_ref, acc_ref, *, alpha):
    j = pl.program_id(1)

    @pl.when(j == 0)
    def _():
        xb_ref[...] = x_ref[...].astype(BF16)
        acc_ref[...] = jnp.zeros(acc_ref.shape, F32)

    xb = xb_ref[...]
    h = _silu(_dot(xb, wg_ref[...])) * _dot(xb, wu_ref[...])
    acc_ref[...] += _dot(h.astype(BF16), wd_ref[...])

    @pl.when(j == pl.num_programs(1) - 1)
    def _():
        y_ref[...] = _layer_norm(alpha * x_ref[...] + acc_ref[...], g_ref[...], b_ref[...])


def _ffn(x, wg, wu, wd, g, b, alpha):
    rows, d = x.shape
    d_ff = wg.shape[1]
    tm = ROW_TILE
    tf = 256 if d_ff % 256 == 0 else LANES
    vec = pl.BlockSpec((1, d), lambda i, j: (0, 0))
    return pl.pallas_call(
        functools.partial(_ffn_kernel, alpha=alpha),
        grid=(rows // tm, d_ff // tf),
        in_specs=[pl.BlockSpec((tm, d), lambda i, j: (i, 0)),
                  pl.BlockSpec((d, tf), lambda i, j: (0, j)), pl.BlockSpec((d, tf), lambda i, j: (0, j)),
                  pl.BlockSpec((tf, d), lambda i, j: (j, 0)), vec, vec],
        out_specs=pl.BlockSpec((tm, d), lambda i, j: (i, 0)),
        out_shape=jax.ShapeDtypeStruct((rows, d), F32),
        scratch_shapes=[pltpu.VMEM((tm, d), BF16), pltpu.VMEM((tm, d), F32)],
        compiler_params=_params("parallel", "arbitrary"),
        name="ffn",
    )(x, wg, wu, wd, g, b)


def _rope_tables(pos):
    half = HEAD_DIM // 2
    inv = ROPE_THETA ** (-jnp.arange(half, dtype=F32) / half)
    ang = pos.astype(F32)[:, None] * inv[None, :]
    cos, sin = jnp.cos(ang), jnp.sin(ang)
    reps = LANES // HEAD_DIM
    return (jnp.tile(jnp.concatenate([cos, cos], -1), (1, reps)),
            jnp.tile(jnp.concatenate([-sin, sin], -1), (1, reps)))


def _overlap_t(n_cmp, n_slc, rows, cols):
    i = np.arange(n_cmp)[:, None]
    j = np.arange(n_slc)[None, :]
    ov = np.minimum(i * CMP_STRIDE + CMP_LEN, (j + 1) * SLC_BLK) - np.maximum(i * CMP_STRIDE, j * SLC_BLK)
    m = np.zeros((rows, cols), np.float32)
    m[:n_slc, :n_cmp] = (np.clip(ov, 0, None) / CMP_LEN).T
    return jnp.asarray(m, dtype=BF16)


def _expand_table(rows, n_keys):
    e = (np.arange(n_keys)[None, :] // SLC_BLK) == np.arange(rows)[:, None]
    return jnp.asarray(e.astype(np.float32), dtype=BF16)


def _compress_weights(pe, w1, w2):
    r = CMP_LEN // CMP_STRIDE
    eye = jnp.eye(N_KV, dtype=F32)
    w1_r = w1.reshape(r, CMP_STRIDE, HEAD_DIM, CMP_HID)
    w1b = jnp.einsum('msde,gh->sgdmhe', w1_r, eye).reshape(CMP_STRIDE * KV_W, r * N_KV * CMP_HID).astype(BF16)
    w2b = jnp.einsum('ed,gh->gehd', w2, eye).reshape(N_KV * CMP_HID, KV_W).astype(BF16)
    pe_r = pe.reshape(r, CMP_STRIDE, 1, HEAD_DIM)
    pe2 = jnp.broadcast_to(pe_r, (r, CMP_STRIDE, N_KV, HEAD_DIM)).reshape(r, CMP_STRIDE * KV_W)
    pe2 = jnp.pad(pe2, ((0, SUBLANES - r), (0, 0)))
    return pe2, w1b, w2b


def _pad_rows(a, n):
    return jnp.pad(a, ((0, 0), (0, n - a.shape[1]), (0, 0)))


def kernel(x_prompt, x_sample, cache_cmp_k, cache_cmp_v, cache_slc_k, cache_slc_v, state_win_k, state_win_v, state_conv, page_table, w_in, cmp_pe_k, cmp_w1_k, cmp_w2_k, cmp_pe_v, cmp_w1_v, cmp_w2_v, conv_w, conv_b, conv_ln_g, conv_ln_b, w_conv_out, w_attn_out, w_o, ln1_g, ln1_b, w_gate, w_up, w_down, ln2_g, ln2_b):
    n_seq, t_len, d = x_prompt.shape
    db, dt, _ = x_sample.shape
    depth = w_in.shape[0]
    page = cache_cmp_k.shape[2]
    past = page_table.shape[1] * page
    wbuf = state_win_k.shape[2]
    alpha = (2.0 * depth) ** 0.25
    np_rows = n_seq * t_len
    ns_rows = db * dt
    assert t_len % ROW_TILE == 0 and ns_rows <= ROW_TILE and dt <= DEC_PAD
    assert QW == 4 * LANES and past % CMP_STRIDE == 0 and past >= CMP_LEN
    n_prompt_tiles = np_rows // ROW_TILE
    tiles_per_seq = t_len // ROW_TILE

    x = jnp.concatenate([x_prompt.reshape(np_rows, d), x_sample.reshape(ns_rows, d),
                         jnp.zeros((ROW_TILE - ns_rows, d), F32)], axis=0)
    pos = jnp.concatenate([jnp.arange(t_len), past + (jnp.arange(ROW_TILE) % dt)])
    cos_t, sin_t = _rope_tables(pos)

    n_cmp_p = (t_len - CMP_LEN) // CMP_STRIDE + 1
    n_slc_p = -(-t_len // SLC_BLK)
    assert n_slc_p <= SEL_ROWS_PROMPT and t_len // CMP_STRIDE <= LANES
    mt_p = _overlap_t(n_cmp_p, n_slc_p, SEL_ROWS_PROMPT, t_len // CMP_STRIDE)
    exp_p = _expand_table(SEL_ROWS_PROMPT, t_len).reshape(SEL_ROWS_PROMPT, t_len // SLC_TK, SLC_TK).transpose(1, 0, 2)
    n_all = past + dt
    n_cmp_s = (n_all - CMP_LEN) // CMP_STRIDE + 1
    n_slc_s = -(-n_all // SLC_BLK)
    n_ch_s = past // CMP_STRIDE
    assert n_cmp_s + 1 <= n_ch_s and n_slc_s <= SEL_ROWS_SAMPLE and (n_slc_s - 1) * SLC_BLK == past
    mt_s = _overlap_t(n_cmp_s, n_slc_s, SEL_ROWS_SAMPLE, n_ch_s)
    exp_s = _expand_table(SEL_ROWS_SAMPLE, past)

    bf = lambda a: a.astype(BF16)
    outs_p = [[] for _ in range(7)]
    outs_s = [[] for _ in range(7)]
    for l in range(depth):
        w_z = bf(w_in[l, :, :ZW])
        w_ga = bf(w_in[l, :, OFF_GA:OFF_GA + d])
        w_gb = bf(w_in[l, :, OFF_GA + d:OFF_GA + 2 * d])
        u, q, qr, kc, vc, ks, vs, kw, vw, gates = _proj(x, w_z, cos_t, sin_t, n_prompt_tiles, tiles_per_seq)

        smp = lambda a: a[np_rows:np_rows + ns_rows].reshape(db, dt, a.shape[-1])
        smp8 = lambda a: _pad_rows(smp(a), DEC_PAD)
        row2 = lambda v: v.reshape(1, -1)

        cw, cb, cg, cbeta = conv_w[l], row2(conv_b[l]), row2(conv_ln_g[l]), row2(conv_ln_b[l])
        c_p = _conv_prompt(u, cw, cb, cg, cbeta, n_seq, t_len)
        u_s = smp(u)
        c_s = _conv_sample(state_conv[l].transpose(1, 0, 2), u_s.transpose(1, 0, 2), cw, cb, cg, cbeta)
        c_s = jnp.pad(c_s.transpose(1, 0, 2).reshape(ns_rows, C_CONV), ((0, ROW_TILE - ns_rows), (0, 0)))

        pe2k, w1k, w2k = _compress_weights(cmp_pe_k[l], cmp_w1_k[l], cmp_w2_k[l])
        pe2v, w1v, w2v = _compress_weights(cmp_pe_v[l], cmp_w1_v[l], cmp_w2_v[l])
        kcmp_p = _compress_prompt(kc, pe2k, w1k, w2k, n_seq, t_len)
        vcmp_p = _compress_prompt(vc, pe2v, w1v, w2v, n_seq, t_len)
        kcmp_s = _compress_sample(page_table, cache_cmp_k[l], pe2k, w1k, w2k)
        vcmp_s = _compress_sample(page_table, cache_cmp_v[l], pe2v, w1v, w2v)

        o_cmp, sel_p = _cmp_sel_prompt(q, kcmp_p, vcmp_p, gates, mt_p, n_seq, t_len, n_cmp_p, n_slc_p)
        o_slc = _slc_prompt(qr, ks, vs, sel_p, exp_p, gates, n_seq, t_len)
        o_win = _win_prompt(qr, kw, vw, gates, n_seq, t_len)

        ks_s, vs_s, kw_s, vw_s = smp(ks), smp(vs), smp(kw), smp(vw)
        q8, qr8, g8 = smp8(q), smp8(qr), smp8(gates)
        wk = state_win_k[l].reshape(db, wbuf, KV_W)
        wv = state_win_v[l].reshape(db, wbuf, KV_W)
        part, sel_s = _cmp_win_sample(q8, qr8, kcmp_s, vcmp_s, wk, wv, _pad_rows(kw_s, DEC_PAD), _pad_rows(vw_s, DEC_PAD),
                                      g8, mt_s, past, n_cmp_s, n_slc_s, dt)
        o_s8 = _slc_sample(page_table, qr8, sel_s, _pad_rows(ks_s, DEC_PAD), _pad_rows(vs_s, DEC_PAD), g8, part, exp_s,
                           cache_slc_k[l], cache_slc_v[l], dt)
        o_s = jnp.pad(o_s8[:, :dt].reshape(ns_rows, QW), ((0, ROW_TILE - ns_rows), (0, 0)))

        x = _merge(x, c_p, c_s, o_cmp, o_slc, o_win, o_s, w_ga, w_gb, bf(w_conv_out[l]), bf(w_attn_out[l]),
                   bf(w_o[l]), row2(ln1_g[l]), row2(ln1_b[l]), alpha)
        x = _ffn(x, bf(w_gate[l]), bf(w_up[l]), bf(w_down[l]), row2(ln2_g[l]), row2(ln2_b[l]), alpha)

        prm = lambda a: a[:np_rows].reshape(n_seq, t_len, N_KV, HEAD_DIM)
        kv4 = lambda a: a.reshape(db, -1, N_KV, HEAD_DIM)
        n_keep = min(WINDOW, t_len)
        new_p = (u[:np_rows].reshape(n_seq, t_len, C_CONV)[:, t_len - (CONV_W - 1):], prm(kc), prm(vc), prm(ks), prm(vs),
                 prm(kw)[:, t_len - n_keep:], prm(vw)[:, t_len - n_keep:])
        new_s = (jnp.concatenate([state_conv[l], u_s], axis=1)[:, -(CONV_W - 1):],
                 kv4(smp(kc)), kv4(smp(vc)), kv4(ks_s), kv4(vs_s),
                 jnp.concatenate([state_win_k[l], kv4(kw_s)], axis=1)[:, dt:],
                 jnp.concatenate([state_win_v[l], kv4(vw_s)], axis=1)[:, dt:])
        for i in range(7):
            outs_p[i].append(new_p[i])
            outs_s[i].append(new_s[i])

    stk = lambda rows: jnp.stack(rows, axis=0)
    order = (1, 2, 3, 4, 5, 6, 0)
    return ((x[:np_rows].reshape(n_seq, t_len, d), x[np_rows:np_rows + ns_rows].reshape(db, dt, d))
            + tuple(stk(outs_p[i]) for i in order) + tuple(stk(outs_s[i]) for i in order))
```

```python
import functools
import math

import numpy as np
import jax
import jax.numpy as jnp
from jax import lax
from jax.experimental import pallas as pl
from jax.experimental.pallas import tpu as pltpu

F32 = jnp.float32
BF16 = jnp.bfloat16

C_CONV = 512
CONV_W = 31
N_HEADS = 8
N_KV = 2
HG = N_HEADS // N_KV
HEAD_DIM = 64
CMP_LEN = 32
CMP_STRIDE = 16
CMP_HID = 64
SLC_BLK = 64
N_SEL = 16
WINDOW = 512
ROPE_THETA = 10000.0
LN_EPS = 1e-5
MASK_VALUE = -1e30
FORCE_SCORE = 1e4
KV_W = N_KV * HEAD_DIM
QW = N_HEADS * HEAD_DIM
SCALE = HEAD_DIM ** -0.5

LANES = 128
SUBLANES = 8
VMEM_LIMIT = 56 * 1024 * 1024

OFF_UA = 0
OFF_UG = C_CONV
OFF_Q = 2 * C_CONV
OFF_KC = OFF_Q + QW
OFF_VC = OFF_KC + KV_W
OFF_KS = OFF_VC + KV_W
OFF_VS = OFF_KS + KV_W
OFF_KW = OFF_VS + KV_W
OFF_VW = OFF_KW + KV_W
OFF_GN = OFF_VW + KV_W
OFF_GA = OFF_GN + 3 * N_HEADS
ZW = OFF_GN + LANES

ROW_TILE = 512
MERGE_TILE = 256
DEC_PAD = 8
ATT_TQ = 128
SLC_TK = 512
CONV_TT = 64
SEL_ROWS_PROMPT = 128
SEL_ROWS_SAMPLE = 256


def _params(*sem):
    return pltpu.CompilerParams(dimension_semantics=sem, vmem_limit_bytes=VMEM_LIMIT)


def _sigmoid(x):
    return 1.0 / (1.0 + jnp.exp(-x))


def _silu(x):
    return x * _sigmoid(x)


def _gelu_tanh(x):
    return 0.5 * x * (1.0 + jnp.tanh(math.sqrt(2.0 / math.pi) * (x + 0.044715 * (x * x * x))))


def _layer_norm(y, g, b):
    mu = jnp.mean(y, axis=-1, keepdims=True)
    d = y - mu
    var = jnp.mean(d * d, axis=-1, keepdims=True)
    return d * lax.rsqrt(var + LN_EPS) * g + b


def _dot(a, b):
    return jnp.dot(a, b, preferred_element_type=F32)


def _dot_nt(a, b):
    return lax.dot_general(a, b, (((1,), (1,)), ((), ())), preferred_element_type=F32)


def _split_bf16(a, terms):
    parts, rest = [], a
    for _ in range(terms):
        part = rest.astype(BF16)
        parts.append(part)
        rest = rest - part.astype(F32)
    return parts


def _rope_tile(x, cos, sin_signed):
    lane = lax.broadcasted_iota(jnp.int32, x.shape, 1)
    first = (lane % HEAD_DIM) < (HEAD_DIM // 2)
    rot = jnp.where(first, pltpu.roll(x, LANES - HEAD_DIM // 2, 1), pltpu.roll(x, HEAD_DIM // 2, 1))
    return x * cos + rot * sin_signed


def _stack_heads(q, g):
    tq = q.shape[0]
    in_half = (lax.broadcasted_iota(jnp.int32, (tq, LANES), 1) // HEAD_DIM) == g
    slabs = []
    for h in range(HG):
        hh = g * HG + h
        col = q[:, (hh // 2) * LANES:(hh // 2 + 1) * LANES]
        if hh % 2 != g:
            col = pltpu.roll(col, HEAD_DIM, 1)
        slabs.append(jnp.where(in_half, col * SCALE, 0.0))
    return jnp.concatenate(slabs, axis=0).astype(BF16)


def _assemble_heads(heads):
    left = lax.broadcasted_iota(jnp.int32, heads[0].shape, 1) < HEAD_DIM
    cols = []
    for c in range(N_HEADS // 2):
        pair = []
        for hh in (2 * c, 2 * c + 1):
            a = heads[hh]
            pair.append(a if hh % 2 == hh // HG else pltpu.roll(a, HEAD_DIM, 1))
        cols.append(jnp.where(left, pair[0], pair[1]))
    return cols


def _v_ext(v, g):
    keep = (lax.broadcasted_iota(jnp.int32, v.shape, 1) // HEAD_DIM) == g
    return jnp.where(keep, v, 1.0).astype(BF16)


def _normalise(acc):
    return acc / pltpu.roll(acc, HEAD_DIM, 1)


def _proj_kernel(x_ref, w_ref, cos_ref, sin_ref,
                 u_ref, q_ref, qr_ref, kc_ref, vc_ref, ks_ref, vs_ref, kw_ref, vw_ref, g_ref):
    xb = x_ref[...].astype(BF16)

    def seg(off, n):
        return _dot(xb, w_ref[:, off:off + n])

    cos = cos_ref[...]
    sin = sin_ref[...]
    u_ref[...] = seg(OFF_UA, C_CONV) * _sigmoid(seg(OFF_UG, C_CONV))
    for c in range(QW // LANES):
        qc = seg(OFF_Q + c * LANES, LANES)
        q_ref[:, c * LANES:(c + 1) * LANES] = qc
        qr_ref[:, c * LANES:(c + 1) * LANES] = _rope_tile(qc, cos, sin)
    kc_ref[...] = seg(OFF_KC, KV_W)
    vc_ref[...] = seg(OFF_VC, KV_W)
    ks_ref[...] = _rope_tile(seg(OFF_KS, KV_W), cos, sin)
    vs_ref[...] = seg(OFF_VS, KV_W)
    kw_ref[...] = _rope_tile(seg(OFF_KW, KV_W), cos, sin)
    vw_ref[...] = seg(OFF_VW, KV_W)
    g_ref[...] = _sigmoid(seg(OFF_GN, LANES))


def _proj(x, w, cos_t, sin_t, n_prompt_tiles, tiles_per_seq):
    rows, d = x.shape
    tm = ROW_TILE

    def tab_map(i):
        return (jnp.where(i < n_prompt_tiles, i % tiles_per_seq, tiles_per_seq), 0)

    row = lambda n: pl.BlockSpec((tm, n), lambda i: (i, 0))
    widths = (C_CONV, QW, QW) + (KV_W,) * 6 + (LANES,)
    return pl.pallas_call(
        _proj_kernel,
        grid=(rows // tm,),
        in_specs=[row(d), pl.BlockSpec((d, ZW), lambda i: (0, 0)),
                  pl.BlockSpec((tm, LANES), tab_map), pl.BlockSpec((tm, LANES), tab_map)],
        out_specs=[row(n) for n in widths],
        out_shape=[jax.ShapeDtypeStruct((rows, n), F32) for n in widths],
        compiler_params=_params("parallel"),
        name="proj_in",
    )(x, w, cos_t, sin_t)


def _conv_prompt_kernel(u_ref, w_ref, b_ref, g_ref, beta_ref, c_ref, ext_ref):
    t_len = u_ref.shape[0]
    hist = ext_ref.shape[0] - t_len
    ext_ref[0:hist, :] = jnp.zeros((hist, C_CONV), F32)
    ext_ref[hist:, :] = u_ref[...]
    first = hist - (CONV_W - 1)

    def body(i, carry):
        t0 = pl.multiple_of(i * CONV_TT, CONV_TT)
        win = ext_ref[pl.ds(t0, CONV_TT + hist), :]
        acc = jnp.zeros((CONV_TT, C_CONV), F32) + b_ref[...]
        for k in range(CONV_W):
            acc = acc + win[first + k:first + k + CONV_TT] * w_ref[k:k + 1, :]
        c_ref[pl.ds(t0, CONV_TT), :] = _silu(_layer_norm(acc, g_ref[...], beta_ref[...]))
        return carry

    lax.fori_loop(0, t_len // CONV_TT, body, 0)


def _conv_prompt(u, w, b, g, beta, n_seq, t_len):
    hist = 32
    vec = pl.BlockSpec((1, C_CONV), lambda i: (0, 0))
    return pl.pallas_call(
        _conv_prompt_kernel,
        grid=(n_seq,),
        in_specs=[pl.BlockSpec((t_len, C_CONV), lambda i: (i, 0)),
                  pl.BlockSpec((CONV_W, C_CONV), lambda i: (0, 0)), vec, vec, vec],
        out_specs=pl.BlockSpec((t_len, C_CONV), lambda i: (i, 0)),
        out_shape=jax.ShapeDtypeStruct((n_seq * t_len, C_CONV), F32),
        scratch_shapes=[pltpu.VMEM((hist + t_len, C_CONV), F32)],
        compiler_params=_params("parallel"),
        name="conv_prompt",
    )(u, w, b, g, beta)


def _conv_sample_kernel(st_ref, u_ref, w_ref, b_ref, g_ref, beta_ref, c_ref):
    n_prev = st_ref.shape[0]
    n_new = u_ref.shape[0]
    for t in range(n_new):
        acc = jnp.zeros(c_ref.shape[1:], F32) + b_ref[...]
        for k in range(CONV_W):
            j = t + k
            row = st_ref[j] if j < n_prev else u_ref[j - n_prev]
            acc = acc + row * w_ref[k:k + 1, :]
        c_ref[t] = _silu(_layer_norm(acc, g_ref[...], beta_ref[...]))


def _conv_sample(state_t, u_t, w, b, g, beta):
    return pl.pallas_call(
        _conv_sample_kernel,
        out_shape=jax.ShapeDtypeStruct(u_t.shape, F32),
        compiler_params=pltpu.CompilerParams(vmem_limit_bytes=VMEM_LIMIT),
        name="conv_sample",
    )(state_t, u_t, w, b, g, beta)


def _compress_core(ch, pe_ref, w1_ref, w2_ref):
    n = ch.shape[0]
    hcat = _dot(ch.astype(BF16), w1_ref[...])
    pcat = sum(_dot(part, w1_ref[...]) for part in _split_bf16(pe_ref[...], 2))
    h0 = hcat[:, :LANES] + pcat[0:1, :LANES]
    h1 = hcat[:, LANES:] + pcat[1:2, LANES:]
    h = h0 + pltpu.roll(h1, n - 1, 0)
    return _dot(_gelu_tanh(h).astype(BF16), w2_ref[...])


def _compress_prompt_kernel(ch_ref, pe_ref, w1_ref, w2_ref, o_ref):
    o_ref[0] = _compress_core(ch_ref[...], pe_ref, w1_ref, w2_ref)


def _compress_prompt(rows, pe2, w1b, w2b, n_seq, t_len):
    n_ch = t_len // CMP_STRIDE
    kdim = CMP_STRIDE * KV_W
    ch = rows.reshape(rows.shape[0] // CMP_STRIDE, kdim)
    return pl.pallas_call(
        _compress_prompt_kernel,
        grid=(n_seq,),
        in_specs=[pl.BlockSpec((n_ch, kdim), lambda i: (i, 0)),
                  pl.BlockSpec((SUBLANES, kdim), lambda i: (0, 0)),
                  pl.BlockSpec((kdim, 2 * LANES), lambda i: (0, 0)),
                  pl.BlockSpec((LANES, LANES), lambda i: (0, 0))],
        out_specs=pl.BlockSpec((1, n_ch, LANES), lambda i: (i, 0, 0)),
        out_shape=jax.ShapeDtypeStruct((n_seq, n_ch, LANES), F32),
        compiler_params=_params("parallel"),
        name="compress_prompt",
    )(ch, pe2, w1b, w2b)


def _pages_native(cache):
    depth, n_pool, page, n_kv, hd = cache.shape
    return jnp.transpose(cache, (0, 1, 3, 4, 2)).reshape(depth, n_pool, n_kv * hd, page)


def _page_copy(cache_ref, buf_ref, sem_ref, layer, page, slot, p):
    return pltpu.make_async_copy(cache_ref.at[layer, page], buf_ref.at[slot, p], sem_ref.at[slot])


def _for_pages(n_pages, need, fn):
    def body(p, carry):
        if need is None:
            fn(p)
        else:
            @pl.when(need(p) != 0)
            def _():
                fn(p)
        return carry
    lax.fori_loop(0, n_pages, body, 0)


def _start_pages(pt_ref, cache_ref, buf_ref, sem_ref, layer, seq, slot, n_pages, need_ref=None):
    need = None if need_ref is None else (lambda p: need_ref[seq, p])
    _for_pages(n_pages, need,
               lambda p: _page_copy(cache_ref, buf_ref, sem_ref, layer, pt_ref[seq, p], slot, p).start())


def _wait_pages(pt_ref, cache_ref, buf_ref, sem_ref, layer, seq, slot, n_pages, need_ref=None):
    need = None if need_ref is None else (lambda p: need_ref[seq, p])
    _for_pages(n_pages, need,
               lambda p: _page_copy(cache_ref, buf_ref, sem_ref, layer, pt_ref[seq, p], slot, p).wait())


def _compress_sample_kernel(pt_ref, cache_ref, pe_ref, w1_ref, w2_ref, o_ref, buf_ref, rows_ref, ch_ref, sem_ref,
                            *, layer):
    b = pl.program_id(0)
    nb = pl.num_programs(0)
    n_pages, page = buf_ref.shape[1], buf_ref.shape[3]
    n_ch = ch_ref.shape[0]
    slot = b % 2

    @pl.when(b == 0)
    def _():
        _start_pages(pt_ref, cache_ref, buf_ref, sem_ref, layer, 0, 0, n_pages)

    @pl.when(b + 1 < nb)
    def _():
        _start_pages(pt_ref, cache_ref, buf_ref, sem_ref, layer, b + 1, 1 - slot, n_pages)

    _wait_pages(pt_ref, cache_ref, buf_ref, sem_ref, layer, b, slot, n_pages)

    def transpose_page(p, carry):
        rows_ref[pl.ds(pl.multiple_of(p * page, page), page), :] = buf_ref[slot, p].T
        return carry
    lax.fori_loop(0, n_pages, transpose_page, 0)
    for s in range(CMP_STRIDE):
        ch_ref[:, s * KV_W:(s + 1) * KV_W] = rows_ref[pl.ds(s, n_ch, stride=CMP_STRIDE), :].astype(BF16)
    o_ref[0] = _compress_core(ch_ref[...], pe_ref, w1_ref, w2_ref)


def _compress_sample(page_table, cache_n, pe2, w1b, w2b, layer):
    page = cache_n.shape[3]
    kdim = CMP_STRIDE * KV_W
    n_seq, n_pages = page_table.shape
    n_ch = n_pages * page // CMP_STRIDE
    grid_spec = pltpu.PrefetchScalarGridSpec(
        num_scalar_prefetch=1,
        grid=(n_seq,),
        in_specs=[pl.BlockSpec(memory_space=pl.ANY),
                  pl.BlockSpec((SUBLANES, kdim), lambda i, pt: (0, 0)),
                  pl.BlockSpec((kdim, 2 * LANES), lambda i, pt: (0, 0)),
                  pl.BlockSpec((LANES, LANES), lambda i, pt: (0, 0))],
        out_specs=pl.BlockSpec((1, n_ch, LANES), lambda i, pt: (i, 0, 0)),
        scratch_shapes=[pltpu.VMEM((2, n_pages, KV_W, page), F32), pltpu.VMEM((n_pages * page, KV_W), F32),
                        pltpu.VMEM((n_ch, kdim), BF16), pltpu.SemaphoreType.DMA((2,))],
    )
    return pl.pallas_call(
        functools.partial(_compress_sample_kernel, layer=layer),
        grid_spec=grid_spec,
        out_shape=jax.ShapeDtypeStruct((n_seq, n_ch, LANES), F32),
        compiler_params=_params("arbitrary"),
        name="compress_sample",
    )(page_table, cache_n, pe2, w1b, w2b)


def _select_mask(score_t, qpos_row, n_slc):
    p_rows = score_t.shape[0]
    j = lax.broadcasted_iota(jnp.int32, score_t.shape, 0)
    cur = qpos_row // SLC_BLK
    forced = (j == 0) | (j == cur) | (j == cur - 1)
    sc = jnp.where(forced, FORCE_SCORE, jnp.where(j <= cur, score_t, -1.0))
    sc = jnp.where(j < n_slc, sc, -2.0)
    sel = jnp.zeros(score_t.shape, F32)
    for _ in range(min(N_SEL, n_slc)):
        mx = jnp.max(sc, axis=0, keepdims=True)
        idx = jnp.min(jnp.where(sc == mx, j, p_rows), axis=0, keepdims=True)
        hit = j == idx
        sel = jnp.where(hit, 1.0, sel)
        sc = jnp.where(hit, -3.0, sc)
    return sel


def _cmp_probs(qh, kg, mask, any_row):
    s = _dot_nt(qh.astype(BF16), kg) * SCALE
    s = jnp.where(mask, s, MASK_VALUE)
    e = jnp.exp(s - jnp.max(s, axis=-1, keepdims=True))
    return e / jnp.sum(e, axis=-1, keepdims=True) * any_row


def _cmp_sel_prompt_kernel(q_ref, kc_ref, vc_ref, g_ref, mt_ref, o_ref, sel_ref, *, n_cmp, n_slc):
    tq = q_ref.shape[0]
    n_pad = kc_ref.shape[1]
    q0 = pl.program_id(1) * tq
    qpos_col = q0 + lax.broadcasted_iota(jnp.int32, (tq, 1), 0)
    qpos_row = q0 + lax.broadcasted_iota(jnp.int32, (1, tq), 1)
    n_idx = lax.broadcasted_iota(jnp.int32, (tq, n_pad), 1)
    mask = (n_idx * CMP_STRIDE + (CMP_LEN - 1) <= qpos_col) & (n_idx < n_cmp)
    any_row = (qpos_col >= CMP_LEN - 1).astype(F32)
    gates = g_ref[...]
    q = q_ref[...]
    kb = kc_ref[0].astype(BF16)
    vb = vc_ref[0].astype(BF16)
    all_causal = (q0 + tq - 1) // SLC_BLK < min(N_SEL, n_slc)
    blk = lax.broadcasted_iota(jnp.int32, (tq, sel_ref.shape[2]), 1)
    heads = [None] * N_HEADS
    for g in range(N_KV):
        s = _dot_nt(_stack_heads(q, g), kb)
        probs = []
        for h in range(HG):
            sh = jnp.where(mask, s[h * tq:(h + 1) * tq], MASK_VALUE)
            e = jnp.exp(sh - jnp.max(sh, axis=-1, keepdims=True))
            probs.append(e / jnp.sum(e, axis=-1, keepdims=True) * any_row)
        o = _dot(jnp.concatenate([p.astype(BF16) for p in probs], axis=0), vb)
        for h in range(HG):
            hh = g * HG + h
            heads[hh] = o[h * tq:(h + 1) * tq] * gates[:, 3 * hh:3 * hh + 1]
        psum = sum(probs[1:], probs[0])

        @pl.when(all_causal)
        def _():
            sel_ref[g] = (blk <= qpos_col // SLC_BLK).astype(F32)

        @pl.when(jnp.logical_not(all_causal))
        def _():
            score_t = sum(_dot_nt(mt_ref[...], part) for part in _split_bf16(psum, 3))
            sel_ref[g] = _select_mask(score_t, qpos_row, n_slc).T

    for c, col in enumerate(_assemble_heads(heads)):
        o_ref[:, c * LANES:(c + 1) * LANES] = col


def _cmp_sel_prompt(q, kcmp, vcmp, gates, mt, n_seq, t_len, n_cmp, n_slc):
    tq = ATT_TQ
    nq = t_len // tq
    n_pad = kcmp.shape[1]
    rows = n_seq * t_len
    kv = pl.BlockSpec((1, n_pad, LANES), lambda b, i: (b, 0, 0))
    return pl.pallas_call(
        functools.partial(_cmp_sel_prompt_kernel, n_cmp=n_cmp, n_slc=n_slc),
        grid=(n_seq, nq),
        in_specs=[pl.BlockSpec((tq, QW), lambda b, i: (b * nq + i, 0)), kv, kv,
                  pl.BlockSpec((tq, LANES), lambda b, i: (b * nq + i, 0)),
                  pl.BlockSpec(mt.shape, lambda b, i: (0, 0))],
        out_specs=[pl.BlockSpec((tq, QW), lambda b, i: (b * nq + i, 0)),
                   pl.BlockSpec((N_KV, tq, SEL_ROWS_PROMPT), lambda b, i: (0, b * nq + i, 0))],
        out_shape=[jax.ShapeDtypeStruct((rows, QW), F32),
                   jax.ShapeDtypeStruct((N_KV, rows, SEL_ROWS_PROMPT), F32)],
        compiler_params=_params("parallel", "parallel"),
        name="cmp_select_prompt",
    )(q, kcmp, vcmp, gates, mt)


def _slc_prompt_kernel(qr_ref, ks_ref, vs_ref, sel_ref, e_ref, g_ref, o_ref, qs_ref, m_ref, acc_ref):
    tq = qr_ref.shape[0]
    tk = e_ref.shape[2]
    q0 = pl.program_id(1) * tq
    qpos = q0 + lax.broadcasted_iota(jnp.int32, (tq, 1), 0)
    q = qr_ref[...]
    for g in range(N_KV):
        qs_ref[g] = _stack_heads(q, g)
    m_ref[...] = jnp.full(m_ref.shape, MASK_VALUE, F32)
    acc_ref[...] = jnp.zeros(acc_ref.shape, F32)

    def body(c, carry):
        k0 = pl.multiple_of(c * tk, tk)
        causal = (k0 + lax.broadcasted_iota(jnp.int32, (tq, tk), 1)) <= qpos
        kb = ks_ref[pl.ds(k0, tk), :].astype(BF16)
        v = vs_ref[pl.ds(k0, tk), :]
        for g in range(N_KV):
            chosen = _dot(sel_ref[g].astype(BF16), e_ref[c])
            mask = causal & (chosen > 0.5)
            s = _dot_nt(qs_ref[g], kb)
            probs, alphas = [], []
            for h in range(HG):
                hh = g * HG + h
                sh = jnp.where(mask, s[h * tq:(h + 1) * tq], MASK_VALUE)
                m_old = m_ref[hh]
                m_new = jnp.maximum(m_old, jnp.max(sh, axis=-1, keepdims=True))
                m_ref[hh] = m_new
                probs.append(jnp.exp(sh - jnp.concatenate([m_new] * (tk // LANES), axis=1)).astype(BF16))
                alphas.append(jnp.exp(m_old - m_new))
            pv = _dot(jnp.concatenate(probs, axis=0), _v_ext(v, g))
            for h in range(HG):
                hh = g * HG + h
                acc_ref[hh] = alphas[h] * acc_ref[hh] + pv[h * tq:(h + 1) * tq]
        return carry

    lax.fori_loop(0, (q0 + tq + tk - 1) // tk, body, 0)
    gates = g_ref[...]
    heads = [_normalise(acc_ref[hh]) * gates[:, 3 * hh + 1:3 * hh + 2] for hh in range(N_HEADS)]
    for c, col in enumerate(_assemble_heads(heads)):
        o_ref[:, c * LANES:(c + 1) * LANES] = col


def _slc_prompt(qr, ks, vs, sel, expand, gates, n_seq, t_len):
    tq = ATT_TQ
    nq = t_len // tq
    rows = n_seq * t_len
    kv = pl.BlockSpec((t_len, KV_W), lambda b, i: (b, 0))
    return pl.pallas_call(
        _slc_prompt_kernel,
        grid=(n_seq, nq),
        in_specs=[pl.BlockSpec((tq, QW), lambda b, i: (b * nq + i, 0)), kv, kv,
                  pl.BlockSpec((N_KV, tq, SEL_ROWS_PROMPT), lambda b, i: (0, b * nq + i, 0)),
                  pl.BlockSpec(expand.shape, lambda b, i: (0, 0, 0)),
                  pl.BlockSpec((tq, LANES), lambda b, i: (b * nq + i, 0))],
        out_specs=pl.BlockSpec((tq, QW), lambda b, i: (b * nq + i, 0)),
        out_shape=jax.ShapeDtypeStruct((rows, QW), F32),
        scratch_shapes=[pltpu.VMEM((N_KV, HG * tq, LANES), BF16),
                        pltpu.VMEM((N_HEADS, tq, LANES), F32), pltpu.VMEM((N_HEADS, tq, LANES), F32)],
        compiler_params=_params("parallel", "parallel"),
        name="slc_prompt",
    )(qr, ks, vs, sel, expand, gates)


def _win_prompt_kernel(qr_ref, kw_ref, vw_ref, g_ref, o_ref, *, nk):
    tq = qr_ref.shape[0]
    q0 = pl.program_id(1) * tq
    start = pl.multiple_of(jnp.maximum(q0 + tq - nk, 0), tq)
    qpos = q0 + lax.broadcasted_iota(jnp.int32, (tq, 1), 0)
    diff = qpos - (start + lax.broadcasted_iota(jnp.int32, (tq, nk), 1))
    mask = (diff >= 0) & (diff < WINDOW)
    gates = g_ref[...]
    q = qr_ref[...]
    kb = kw_ref[pl.ds(start, nk), :].astype(BF16)
    v = vw_ref[pl.ds(start, nk), :]
    heads = [None] * N_HEADS
    for g in range(N_KV):
        s = _dot_nt(_stack_heads(q, g), kb)
        probs = []
        for h in range(HG):
            sh = jnp.where(mask, s[h * tq:(h + 1) * tq], MASK_VALUE)
            probs.append(jnp.exp(sh - jnp.max(sh, axis=-1, keepdims=True)).astype(BF16))
        pv = _dot(jnp.concatenate(probs, axis=0), _v_ext(v, g))
        for h in range(HG):
            hh = g * HG + h
            heads[hh] = _normalise(pv[h * tq:(h + 1) * tq]) * gates[:, 3 * hh + 2:3 * hh + 3]
    for c, col in enumerate(_assemble_heads(heads)):
        o_ref[:, c * LANES:(c + 1) * LANES] = col


def _win_prompt(qr, kw, vw, gates, n_seq, t_len):
    tq = ATT_TQ
    nq = t_len // tq
    nk = min(WINDOW + tq, t_len)
    kv = pl.BlockSpec((t_len, KV_W), lambda b, i: (b, 0))
    return pl.pallas_call(
        functools.partial(_win_prompt_kernel, nk=nk),
        grid=(n_seq, nq),
        in_specs=[pl.BlockSpec((tq, QW), lambda b, i: (b * nq + i, 0)), kv, kv,
                  pl.BlockSpec((tq, LANES), lambda b, i: (b * nq + i, 0))],
        out_specs=pl.BlockSpec((tq, QW), lambda b, i: (b * nq + i, 0)),
        out_shape=jax.ShapeDtypeStruct((n_seq * t_len, QW), F32),
        compiler_params=_params("parallel", "parallel"),
        name="win_prompt",
    )(qr, kw, vw, gates)


def _cmp_win_sample_kernel(q_ref, qr_ref, kc_ref, vc_ref, wk_ref, wv_ref, nk_ref, nv_ref, g_ref,
                           o_ref, ps_ref, *, past, n_cmp, n_new):
    nq = q_ref.shape[1]
    n_pad = kc_ref.shape[1]
    wbuf = wk_ref.shape[1]
    qpos_col = past + lax.broadcasted_iota(jnp.int32, (nq, 1), 0)
    n_idx = lax.broadcasted_iota(jnp.int32, (nq, n_pad), 1)
    mask_c = (n_idx * CMP_STRIDE + (CMP_LEN - 1) <= qpos_col) & (n_idx < n_cmp)
    any_row = (qpos_col >= CMP_LEN - 1).astype(F32)
    t_col = lax.broadcasted_iota(jnp.int32, (nq, 1), 0)
    diff_buf = t_col + wbuf - lax.broadcasted_iota(jnp.int32, (nq, wbuf), 1)
    mask_buf = (diff_buf >= 0) & (diff_buf < WINDOW)
    r_idx = lax.broadcasted_iota(jnp.int32, (nq, nq), 1)
    mask_new = (t_col - r_idx >= 0) & (t_col - r_idx < WINDOW) & (r_idx < n_new)
    gates = g_ref[0]
    q = q_ref[0]
    qr = qr_ref[0]
    kcb = kc_ref[0].astype(BF16)
    vcb = vc_ref[0].astype(BF16)
    wkb = wk_ref[0].astype(BF16)
    nkb = nk_ref[0].astype(BF16)
    heads = [None] * N_HEADS
    rows = lambda a, h: a[h * nq:(h + 1) * nq]
    stack = lambda parts: jnp.concatenate(parts, axis=0).astype(BF16)
    for g in range(N_KV):
        sc = _dot_nt(_stack_heads(q, g), kcb)
        qs = _stack_heads(qr, g)
        s1 = _dot_nt(qs, wkb)
        s2 = _dot_nt(qs, nkb)
        probs, e1s, e2s = [], [], []
        for h in range(HG):
            sh = jnp.where(mask_c, rows(sc, h), MASK_VALUE)
            e = jnp.exp(sh - jnp.max(sh, axis=-1, keepdims=True))
            probs.append(e / jnp.sum(e, axis=-1, keepdims=True) * any_row)
            a1 = jnp.where(mask_buf, rows(s1, h), MASK_VALUE)
            a2 = jnp.where(mask_new, rows(s2, h), MASK_VALUE)
            m = jnp.maximum(jnp.max(a1, axis=-1, keepdims=True), jnp.max(a2, axis=-1, keepdims=True))
            e1s.append(jnp.exp(a1 - m))
            e2s.append(jnp.exp(a2 - m))
        o_cmp = _dot(stack(probs), vcb)
        o_win = _normalise(_dot(stack(e1s), _v_ext(wv_ref[0], g)) + _dot(stack(e2s), _v_ext(nv_ref[0], g)))
        for h in range(HG):
            hh = g * HG + h
            heads[hh] = (rows(o_cmp, h) * gates[:, 3 * hh:3 * hh + 1]
                         + rows(o_win, h) * gates[:, 3 * hh + 2:3 * hh + 3])
        ps_ref[0, g] = sum(probs[1:], probs[0])
    for c, col in enumerate(_assemble_heads(heads)):
        o_ref[0, :, c * LANES:(c + 1) * LANES] = col


def _cmp_win_sample(q, qr, kcmp, vcmp, wk, wv, nk, nv, gates, past, n_cmp, n_new):
    n_seq, nq, _ = q.shape
    n_pad = kcmp.shape[1]
    per_seq = lambda a: pl.BlockSpec((1,) + a.shape[1:], lambda b: (b, 0, 0))
    return pl.pallas_call(
        functools.partial(_cmp_win_sample_kernel, past=past, n_cmp=n_cmp, n_new=n_new),
        grid=(n_seq,),
        in_specs=[per_seq(a) for a in (q, qr, kcmp, vcmp, wk, wv, nk, nv, gates)],
        out_specs=[pl.BlockSpec((1, nq, QW), lambda b: (b, 0, 0)),
                   pl.BlockSpec((1, N_KV, nq, n_pad), lambda b: (b, 0, 0, 0))],
        out_shape=[jax.ShapeDtypeStruct((n_seq, nq, QW), F32),
                   jax.ShapeDtypeStruct((n_seq, N_KV, nq, n_pad), F32)],
        compiler_params=_params("parallel"),
        name="cmp_win_sample",
    )(q, qr, kcmp, vcmp, wk, wv, nk, nv, gates)


def _select_sample_kernel(ps_ref, mt_ref, e_ref, selx_ref, *, past, n_slc, n_new):
    nq_all = ps_ref.shape[1]
    n_pages, page = selx_ref.shape[1], selx_ref.shape[3]
    tok_row = lax.broadcasted_iota(jnp.int32, (1, nq_all), 1) % DEC_PAD
    live = (lax.broadcasted_iota(jnp.int32, (nq_all, 1), 0) % DEC_PAD < n_new).astype(F32)
    for g in range(N_KV):
        score_t = sum(_dot_nt(mt_ref[...], part) for part in _split_bf16(ps_ref[g], 3))
        sel = (_select_mask(score_t, past + tok_row, n_slc).T * live).astype(BF16)
        for p in range(n_pages):
            selx_ref[g, p] = _dot(sel, e_ref[:, p * page:(p + 1) * page])


def _select_sample(psum, mt, expand, n_pages, page, past, n_slc, n_new):
    return pl.pallas_call(
        functools.partial(_select_sample_kernel, past=past, n_slc=n_slc, n_new=n_new),
        out_shape=jax.ShapeDtypeStruct((N_KV, n_pages, psum.shape[1], page), F32),
        compiler_params=pltpu.CompilerParams(vmem_limit_bytes=VMEM_LIMIT),
        name="select_sample",
    )(psum, mt, expand)


def _slc_sample_kernel(pt_ref, need_ref, qr_ref, selx_ref, nk_ref, nv_ref, g_ref, part_ref, kcache_ref, vcache_ref,
                       o_ref, kbuf_ref, vbuf_ref, s_ref, acc_ref, ksem_ref, vsem_ref, *, layer, n_new):
    b = pl.program_id(0)
    nb = pl.num_programs(0)
    n_pages = kbuf_ref.shape[1]
    nq = qr_ref.shape[1]
    slot = b % 2

    def start(seq, sl):
        _start_pages(pt_ref, kcache_ref, kbuf_ref, ksem_ref, layer, seq, sl, n_pages, need_ref)
        _start_pages(pt_ref, vcache_ref, vbuf_ref, vsem_ref, layer, seq, sl, n_pages, need_ref)

    @pl.when(b == 0)
    def _():
        start(0, 0)
        s_ref[...] = jnp.zeros(s_ref.shape, F32)

    @pl.when(b + 1 < nb)
    def _():
        start(b + 1, 1 - slot)

    _wait_pages(pt_ref, kcache_ref, kbuf_ref, ksem_ref, layer, b, slot, n_pages, need_ref)
    _wait_pages(pt_ref, vcache_ref, vbuf_ref, vsem_ref, layer, b, slot, n_pages, need_ref)

    q = qr_ref[0]
    qs = jnp.concatenate([_stack_heads(q, g) for g in range(N_KV)], axis=0)
    need = lambda p: need_ref[b, p]

    def score_page(p):
        s_ref[p] = _dot(qs, kbuf_ref[slot, p].astype(BF16))
    _for_pages(n_pages, need, score_page)

    t_col = lax.broadcasted_iota(jnp.int32, (nq, 1), 0)
    r_idx = lax.broadcasted_iota(jnp.int32, (nq, nq), 1)
    mask_new = (r_idx <= t_col) & (r_idx < n_new)
    s_new = _dot_nt(qs, nk_ref[0].astype(BF16))
    e_new, dens = [], []
    for hh in range(N_HEADS):
        rs = slice(hh * nq, (hh + 1) * nq)
        chosen = selx_ref[hh // HG] > 0.5
        sm = jnp.where(chosen, s_ref[:, rs, :], MASK_VALUE)
        sn = jnp.where(mask_new, s_new[rs], MASK_VALUE)
        m = jnp.maximum(jnp.max(jnp.max(sm, axis=0), axis=-1, keepdims=True), jnp.max(sn, axis=-1, keepdims=True))
        e = jnp.exp(sm - m[None])
        en = jnp.exp(sn - m)
        s_ref[:, rs, :] = e
        e_new.append(en)
        dens.append(jnp.sum(jnp.sum(e, axis=0), axis=-1, keepdims=True) + jnp.sum(en, axis=-1, keepdims=True))

    acc_ref[...] = _dot(jnp.concatenate(e_new, axis=0).astype(BF16), nv_ref[0].astype(BF16))

    def value_page(p):
        acc_ref[...] += _dot_nt(s_ref[p].astype(BF16), vbuf_ref[slot, p].astype(BF16))
    _for_pages(n_pages, need, value_page)

    o = acc_ref[...] / jnp.concatenate(dens, axis=0)
    gates = g_ref[0]
    heads = [o[hh * nq:(hh + 1) * nq] * gates[:, 3 * hh + 1:3 * hh + 2] for hh in range(N_HEADS)]
    for c, col in enumerate(_assemble_heads(heads)):
        o_ref[0, :, c * LANES:(c + 1) * LANES] = part_ref[0, :, c * LANES:(c + 1) * LANES] + col


def _slc_sample(page_table, need, qr, selx, nk, nv, gates, part, kcache_n, vcache_n, layer, n_new):
    n_seq, nq, _ = qr.shape
    n_pages = page_table.shape[1]
    page = kcache_n.shape[3]
    per_seq = lambda a: pl.BlockSpec((1,) + a.shape[1:], lambda b, pt, nd: (b, 0, 0))
    grid_spec = pltpu.PrefetchScalarGridSpec(
        num_scalar_prefetch=2,
        grid=(n_seq,),
        in_specs=[per_seq(qr), pl.BlockSpec((N_KV, n_pages, nq, page), lambda b, pt, nd: (0, 0, b, 0))]
                 + [per_seq(a) for a in (nk, nv, gates, part)]
                 + [pl.BlockSpec(memory_space=pl.ANY), pl.BlockSpec(memory_space=pl.ANY)],
        out_specs=pl.BlockSpec((1, nq, QW), lambda b, pt, nd: (b, 0, 0)),
        scratch_shapes=[pltpu.VMEM((2, n_pages, KV_W, page), F32), pltpu.VMEM((2, n_pages, KV_W, page), F32),
                        pltpu.VMEM((n_pages, N_HEADS * nq, page), F32), pltpu.VMEM((N_HEADS * nq, KV_W), F32),
                        pltpu.SemaphoreType.DMA((2,)), pltpu.SemaphoreType.DMA((2,))],
    )
    return pl.pallas_call(
        functools.partial(_slc_sample_kernel, layer=layer, n_new=n_new),
        grid_spec=grid_spec,
        out_shape=jax.ShapeDtypeStruct((n_seq, nq, QW), F32),
        compiler_params=_params("arbitrary"),
        name="slc_sample",
    )(page_table, need, qr, selx, nk, nv, gates, part, kcache_n, vcache_n)


def _merge_kernel(x_ref, cp_ref, cs_ref, op1_ref, op2_ref, op3_ref, os_ref,
                  wga_ref, wgb_ref, wco_ref, wao_ref, wo_ref, g_ref, b_ref, y_ref, *, n_prompt_tiles, alpha):
    is_prompt = pl.program_id(0) < n_prompt_tiles
    x = x_ref[...]
    xb = x.astype(BF16)
    c = jnp.where(is_prompt, cp_ref[...], cs_ref[...])
    o = jnp.where(is_prompt, op1_ref[...] + op2_ref[...] + op3_ref[...], os_ref[...])
    mixed = _sigmoid(_dot(xb, wga_ref[...])) * _dot(c.astype(BF16), wco_ref[...])
    mixed = mixed + _sigmoid(_dot(xb, wgb_ref[...])) * _dot(o.astype(BF16), wao_ref[...])
    y = _dot(mixed.astype(BF16), wo_ref[...])
    y_ref[...] = _layer_norm(alpha * x + y, g_ref[...], b_ref[...])


def _merge(x, c_p, c_s, o_cmp, o_slc, o_win, o_s, wga, wgb, wco, wao, wo, g, b, alpha):
    rows, d = x.shape
    tm = MERGE_TILE
    n_pt = c_p.shape[0] // tm
    prm = lambda n: pl.BlockSpec((tm, n), lambda i: (jnp.minimum(i, n_pt - 1), 0))
    smp = lambda n: pl.BlockSpec((tm, n), lambda i: (jnp.maximum(i - n_pt, 0), 0))
    full = lambda a: pl.BlockSpec(a.shape, lambda i: (0, 0))
    return pl.pallas_call(
        functools.partial(_merge_kernel, n_prompt_tiles=n_pt, alpha=alpha),
        grid=(rows // tm,),
        in_specs=[pl.BlockSpec((tm, d), lambda i: (i, 0)), prm(C_CONV), smp(C_CONV),
                  prm(QW), prm(QW), prm(QW), smp(QW),
                  full(wga), full(wgb), full(wco), full(wao), full(wo), full(g), full(b)],
        out_specs=pl.BlockSpec((tm, d), lambda i: (i, 0)),
        out_shape=jax.ShapeDtypeStruct((rows, d), F32),
        compiler_params=_params("parallel"),
        name="merge",
    )(x, c_p, c_s, o_cmp, o_slc, o_win, o_s, wga, wgb, wco, wao, wo, g, b)


def _ffn_kernel(x_ref, wg_ref, wu_ref, wd_ref, g_ref, b_ref, y_ref, xb_ref, acc_ref, *, alpha):
    j = pl.program_id(1)

    @pl.when(j == 0)
    def _():
        xb_ref[...] = x_ref[...].astype(BF16)
        acc_ref[...] = jnp.zeros(acc_ref.shape, F32)

    xb = xb_ref[...]
    h = _silu(_dot(xb, wg_ref[...])) * _dot(xb, wu_ref[...])
    acc_ref[...] += _dot(h.astype(BF16), wd_ref[...])

    @pl.when(j == pl.num_programs(1) - 1)
    def _():
        y_ref[...] = _layer_norm(alpha * x_ref[...] + acc_ref[...], g_ref[...], b_ref[...])


def _ffn(x, wg, wu, wd, g, b, alpha):
    rows, d = x.shape
    d_ff = wg.shape[1]
    tm = ROW_TILE
    tf = 256 if d_ff % 256 == 0 else LANES
    vec = pl.BlockSpec((1, d), lambda i, j: (0, 0))
    return pl.pallas_call(
        functools.partial(_ffn_kernel, alpha=alpha),
        grid=(rows // tm, d_ff // tf),
        in_specs=[pl.BlockSpec((tm, d), lambda i, j: (i, 0)),
                  pl.BlockSpec((d, tf), lambda i, j: (0, j)), pl.BlockSpec((d, tf), lambda i, j: (0, j)),
                  pl.BlockSpec((tf, d), lambda i, j: (j, 0)), vec, vec],
        out_specs=pl.BlockSpec((tm, d), lambda i, j: (i, 0)),
        out_shape=jax.ShapeDtypeStruct((rows, d), F32),
        scratch_shapes=[pltpu.VMEM((tm, d), BF16), pltpu.VMEM((tm, d), F32)],
        compiler_params=_params("parallel", "arbitrary"),
        name="ffn",
    )(x, wg, wu, wd, g, b)


def _rope_tables(pos):
    half = HEAD_DIM // 2
    inv = ROPE_THETA ** (-jnp.arange(half, dtype=F32) / half)
    ang = pos.astype(F32)[:, None] * inv[None, :]
    cos, sin = jnp.cos(ang), jnp.sin(ang)
    reps = LANES // HEAD_DIM
    return (jnp.tile(jnp.concatenate([cos, cos], -1), (1, reps)),
            jnp.tile(jnp.concatenate([-sin, sin], -1), (1, reps)))


def _overlap_t(n_cmp, n_slc, rows, cols):
    i = np.arange(n_cmp)[:, None]
    j = np.arange(n_slc)[None, :]
    ov = np.minimum(i * CMP_STRIDE + CMP_LEN, (j + 1) * SLC_BLK) - np.maximum(i * CMP_STRIDE, j * SLC_BLK)
    m = np.zeros((rows, cols), np.float32)
    m[:n_slc, :n_cmp] = (np.clip(ov, 0, None) / CMP_LEN).T
    return jnp.asarray(m, dtype=BF16)


def _expand_table(rows, n_keys):
    e = (np.arange(n_keys)[None, :] // SLC_BLK) == np.arange(rows)[:, None]
    return jnp.asarray(e.astype(np.float32), dtype=BF16)


def _compress_weights(pe, w1, w2):
    r = CMP_LEN // CMP_STRIDE
    eye = jnp.eye(N_KV, dtype=F32)
    w1_r = w1.reshape(r, CMP_STRIDE, HEAD_DIM, CMP_HID)
    w1b = jnp.einsum('msde,gh->sgdmhe', w1_r, eye).reshape(CMP_STRIDE * KV_W, r * N_KV * CMP_HID).astype(BF16)
    w2b = jnp.einsum('ed,gh->gehd', w2, eye).reshape(N_KV * CMP_HID, KV_W).astype(BF16)
    pe_r = pe.reshape(r, CMP_STRIDE, 1, HEAD_DIM)
    pe2 = jnp.broadcast_to(pe_r, (r, CMP_STRIDE, N_KV, HEAD_DIM)).reshape(r, CMP_STRIDE * KV_W)
    pe2 = jnp.pad(pe2, ((0, SUBLANES - r), (0, 0)))
    return pe2, w1b, w2b


def _pad_rows(a, n):
    return jnp.pad(a, ((0, 0), (0, n - a.shape[1]), (0, 0)))


def kernel(x_prompt, x_sample, cache_cmp_k, cache_cmp_v, cache_slc_k, cache_slc_v, state_win_k, state_win_v, state_conv, page_table, w_in, cmp_pe_k, cmp_w1_k, cmp_w2_k, cmp_pe_v, cmp_w1_v, cmp_w2_v, conv_w, conv_b, conv_ln_g, conv_ln_b, w_conv_out, w_attn_out, w_o, ln1_g, ln1_b, w_gate, w_up, w_down, ln2_g, ln2_b):
    n_seq, t_len, d = x_prompt.shape
    db, dt, _ = x_sample.shape
    depth = w_in.shape[0]
    page = cache_cmp_k.shape[2]
    past = page_table.shape[1] * page
    wbuf = state_win_k.shape[2]
    alpha = (2.0 * depth) ** 0.25
    np_rows = n_seq * t_len
    ns_rows = db * dt
    assert t_len % ROW_TILE == 0 and ns_rows <= ROW_TILE and dt <= DEC_PAD
    assert QW == 4 * LANES and past % CMP_STRIDE == 0 and past >= CMP_LEN
    n_prompt_tiles = np_rows // ROW_TILE
    tiles_per_seq = t_len // ROW_TILE

    x = jnp.concatenate([x_prompt.reshape(np_rows, d), x_sample.reshape(ns_rows, d),
                         jnp.zeros((ROW_TILE - ns_rows, d), F32)], axis=0)
    pos = jnp.concatenate([jnp.arange(t_len), past + (jnp.arange(ROW_TILE) % dt)])
    cos_t, sin_t = _rope_tables(pos)

    n_cmp_p = (t_len - CMP_LEN) // CMP_STRIDE + 1
    n_slc_p = -(-t_len // SLC_BLK)
    assert n_slc_p <= SEL_ROWS_PROMPT and t_len // CMP_STRIDE <= LANES
    mt_p = _overlap_t(n_cmp_p, n_slc_p, SEL_ROWS_PROMPT, t_len // CMP_STRIDE)
    exp_p = _expand_table(SEL_ROWS_PROMPT, t_len).reshape(SEL_ROWS_PROMPT, t_len // SLC_TK, SLC_TK).transpose(1, 0, 2)
    n_all = past + dt
    n_cmp_s = (n_all - CMP_LEN) // CMP_STRIDE + 1
    n_slc_s = -(-n_all // SLC_BLK)
    n_ch_s = past // CMP_STRIDE
    assert n_cmp_s + 1 <= n_ch_s and n_slc_s <= SEL_ROWS_SAMPLE and (n_slc_s - 1) * SLC_BLK == past
    mt_s = _overlap_t(n_cmp_s, n_slc_s, SEL_ROWS_SAMPLE, n_ch_s)
    exp_s = _expand_table(SEL_ROWS_SAMPLE, past)
    n_pages = page_table.shape[1]
    assert page % SLC_BLK == 0 and KV_W == LANES
    cmp_k_n, cmp_v_n, slc_k_n, slc_v_n = (_pages_native(c) for c in (cache_cmp_k, cache_cmp_v, cache_slc_k, cache_slc_v))

    bf = lambda a: a.astype(BF16)
    outs_p = [[] for _ in range(7)]
    outs_s = [[] for _ in range(7)]
    for l in range(depth):
        w_z = bf(w_in[l, :, :ZW])
        w_ga = bf(w_in[l, :, OFF_GA:OFF_GA + d])
        w_gb = bf(w_in[l, :, OFF_GA + d:OFF_GA + 2 * d])
        u, q, qr, kc, vc, ks, vs, kw, vw, gates = _proj(x, w_z, cos_t, sin_t, n_prompt_tiles, tiles_per_seq)

        smp = lambda a: a[np_rows:np_rows + ns_rows].reshape(db, dt, a.shape[-1])
        smp8 = lambda a: _pad_rows(smp(a), DEC_PAD)
        row2 = lambda v: v.reshape(1, -1)

        cw, cb, cg, cbeta = conv_w[l], row2(conv_b[l]), row2(conv_ln_g[l]), row2(conv_ln_b[l])
        c_p = _conv_prompt(u, cw, cb, cg, cbeta, n_seq, t_len)
        u_s = smp(u)
        c_s = _conv_sample(state_conv[l].transpose(1, 0, 2), u_s.transpose(1, 0, 2), cw, cb, cg, cbeta)
        c_s = jnp.pad(c_s.transpose(1, 0, 2).reshape(ns_rows, C_CONV), ((0, ROW_TILE - ns_rows), (0, 0)))

        pe2k, w1k, w2k = _compress_weights(cmp_pe_k[l], cmp_w1_k[l], cmp_w2_k[l])
        pe2v, w1v, w2v = _compress_weights(cmp_pe_v[l], cmp_w1_v[l], cmp_w2_v[l])
        kcmp_p = _compress_prompt(kc, pe2k, w1k, w2k, n_seq, t_len)
        vcmp_p = _compress_prompt(vc, pe2v, w1v, w2v, n_seq, t_len)
        kcmp_s = _compress_sample(page_table, cmp_k_n, pe2k, w1k, w2k, l)
        vcmp_s = _compress_sample(page_table, cmp_v_n, pe2v, w1v, w2v, l)

        o_cmp, sel_p = _cmp_sel_prompt(q, kcmp_p, vcmp_p, gates, mt_p, n_seq, t_len, n_cmp_p, n_slc_p)
        o_slc = _slc_prompt(qr, ks, vs, sel_p, exp_p, gates, n_seq, t_len)
        o_win = _win_prompt(qr, kw, vw, gates, n_seq, t_len)

        ks_s, vs_s, kw_s, vw_s = smp(ks), smp(vs), smp(kw), smp(vw)
        q8, qr8, g8 = smp8(q), smp8(qr), smp8(gates)
        wk = state_win_k[l].reshape(db, wbuf, KV_W)
        wv = state_win_v[l].reshape(db, wbuf, KV_W)
        part, psum_s = _cmp_win_sample(q8, qr8, kcmp_s, vcmp_s, wk, wv, _pad_rows(kw_s, DEC_PAD),
                                       _pad_rows(vw_s, DEC_PAD), g8, past, n_cmp_s, dt)
        psum_s = psum_s.transpose(1, 0, 2, 3).reshape(N_KV, db * DEC_PAD, n_ch_s)
        selx = _select_sample(psum_s, mt_s, exp_s, n_pages, page, past, n_slc_s, dt)
        need = (selx.reshape(N_KV, n_pages, db, DEC_PAD * page).max(axis=(0, 3)) > 0.5).T.astype(jnp.int32)
        o_s8 = _slc_sample(page_table, need, qr8, selx, _pad_rows(ks_s, DEC_PAD), _pad_rows(vs_s, DEC_PAD), g8, part,
                           slc_k_n, slc_v_n, l, dt)
        o_s = jnp.pad(o_s8[:, :dt].reshape(ns_rows, QW), ((0, ROW_TILE - ns_rows), (0, 0)))

        x = _merge(x, c_p, c_s, o_cmp, o_slc, o_win, o_s, w_ga, w_gb, bf(w_conv_out[l]), bf(w_attn_out[l]),
                   bf(w_o[l]), row2(ln1_g[l]), row2(ln1_b[l]), alpha)
        x = _ffn(x, bf(w_gate[l]), bf(w_up[l]), bf(w_down[l]), row2(ln2_g[l]), row2(ln2_b[l]), alpha)

        prm = lambda a: a[:np_rows].reshape(n_seq, t_len, N_KV, HEAD_DIM)
        kv4 = lambda a: a.reshape(db, -1, N_KV, HEAD_DIM)
        n_keep = min(WINDOW, t_len)
        new_p = (u[:np_rows].reshape(n_seq, t_len, C_CONV)[:, t_len - (CONV_W - 1):], prm(kc), prm(vc), prm(ks), prm(vs),
                 prm(kw)[:, t_len - n_keep:], prm(vw)[:, t_len - n_keep:])
        new_s = (jnp.concatenate([state_conv[l], u_s], axis=1)[:, -(CONV_W - 1):],
                 kv4(smp(kc)), kv4(smp(vc)), kv4(ks_s), kv4(vs_s),
                 jnp.concatenate([state_win_k[l], kv4(kw_s)], axis=1)[:, dt:],
                 jnp.concatenate([state_win_v[l], kv4(vw_s)], axis=1)[:, dt:])
        for i in range(7):
            outs_p[i].append(new_p[i])
            outs_s[i].append(new_s[i])

    stk = lambda rows: jnp.stack(rows, axis=0)
    order = (1, 2, 3, 4, 5, 6, 0)
    return ((x[:np_rows].reshape(n_seq, t_len, d), x[np_rows:np_rows + ns_rows].reshape(db, dt, d))
            + tuple(stk(outs_p[i]) for i in order) + tuple(stk(outs_s[i]) for i in order))
```

```python
import functools
import math

import numpy as np
import jax
import jax.numpy as jnp
from jax import lax
from jax.experimental import pallas as pl
from jax.experimental.pallas import tpu as pltpu

F32 = jnp.float32
BF16 = jnp.bfloat16

C_CONV = 512
CONV_W = 31
N_HEADS = 8
N_KV = 2
HG = N_HEADS // N_KV
HEAD_DIM = 64
CMP_LEN = 32
CMP_STRIDE = 16
CMP_HID = 64
SLC_BLK = 64
N_SEL = 16
WINDOW = 512
ROPE_THETA = 10000.0
LN_EPS = 1e-5
MASK_VALUE = -1e30
FORCE_SCORE = 1e4
KV_W = N_KV * HEAD_DIM
QW = N_HEADS * HEAD_DIM
SCALE = HEAD_DIM ** -0.5

LANES = 128
SUBLANES = 8
VMEM_LIMIT = 56 * 1024 * 1024

OFF_UA = 0
OFF_UG = C_CONV
OFF_Q = 2 * C_CONV
OFF_KC = OFF_Q + QW
OFF_VC = OFF_KC + KV_W
OFF_KS = OFF_VC + KV_W
OFF_VS = OFF_KS + KV_W
OFF_KW = OFF_VS + KV_W
OFF_VW = OFF_KW + KV_W
OFF_GN = OFF_VW + KV_W
OFF_GA = OFF_GN + 3 * N_HEADS
ZW = OFF_GN + LANES

ROW_TILE = 512
MERGE_TILE = 256
DEC_PAD = 8
ATT_TQ = 128
SLC_TK = 512
CONV_TT = 64
SEL_ROWS_PROMPT = 128
SEL_ROWS_SAMPLE = 256


def _params(*sem):
    return pltpu.CompilerParams(dimension_semantics=sem, vmem_limit_bytes=VMEM_LIMIT)


def _sigmoid(x):
    return 1.0 / (1.0 + jnp.exp(-x))


def _silu(x):
    return x * _sigmoid(x)


def _gelu_tanh(x):
    return 0.5 * x * (1.0 + jnp.tanh(math.sqrt(2.0 / math.pi) * (x + 0.044715 * (x * x * x))))


def _layer_norm(y, g, b):
    mu = jnp.mean(y, axis=-1, keepdims=True)
    d = y - mu
    var = jnp.mean(d * d, axis=-1, keepdims=True)
    return d * lax.rsqrt(var + LN_EPS) * g + b


def _dot(a, b):
    return jnp.dot(a, b, preferred_element_type=F32)


def _dot_nt(a, b):
    return lax.dot_general(a, b, (((1,), (1,)), ((), ())), preferred_element_type=F32)


def _split_bf16(a, terms):
    parts, rest = [], a
    for _ in range(terms):
        part = rest.astype(BF16)
        parts.append(part)
        rest = rest - part.astype(F32)
    return parts


def _rope_tile(x, cos, sin_signed):
    lane = lax.broadcasted_iota(jnp.int32, x.shape, 1)
    first = (lane % HEAD_DIM) < (HEAD_DIM // 2)
    rot = jnp.where(first, pltpu.roll(x, LANES - HEAD_DIM // 2, 1), pltpu.roll(x, HEAD_DIM // 2, 1))
    return x * cos + rot * sin_signed


def _stack_heads(q, g):
    tq = q.shape[0]
    in_half = (lax.broadcasted_iota(jnp.int32, (tq, LANES), 1) // HEAD_DIM) == g
    slabs = []
    for h in range(HG):
        hh = g * HG + h
        col = q[:, (hh // 2) * LANES:(hh // 2 + 1) * LANES]
        if hh % 2 != g:
            col = pltpu.roll(col, HEAD_DIM, 1)
        slabs.append(jnp.where(in_half, col * SCALE, 0.0))
    return jnp.concatenate(slabs, axis=0).astype(BF16)


def _assemble_heads(heads):
    left = lax.broadcasted_iota(jnp.int32, heads[0].shape, 1) < HEAD_DIM
    cols = []
    for c in range(N_HEADS // 2):
        pair = []
        for hh in (2 * c, 2 * c + 1):
            a = heads[hh]
            pair.append(a if hh % 2 == hh // HG else pltpu.roll(a, HEAD_DIM, 1))
        cols.append(jnp.where(left, pair[0], pair[1]))
    return cols


def _v_ext(v, g):
    keep = (lax.broadcasted_iota(jnp.int32, v.shape, 1) // HEAD_DIM) == g
    return jnp.where(keep, v, 1.0).astype(BF16)


def _normalise(acc):
    return acc / pltpu.roll(acc, HEAD_DIM, 1)


def _proj_kernel(x_ref, w_ref, cos_ref, sin_ref,
                 u_ref, q_ref, qr_ref, kc_ref, vc_ref, ks_ref, vs_ref, kw_ref, vw_ref, g_ref):
    xb = x_ref[...].astype(BF16)

    def seg(off, n):
        return _dot(xb, w_ref[:, off:off + n])

    cos = cos_ref[...]
    sin = sin_ref[...]
    u_ref[...] = seg(OFF_UA, C_CONV) * _sigmoid(seg(OFF_UG, C_CONV))
    q = seg(OFF_Q, QW)
    q_ref[...] = q
    for c in range(QW // LANES):
        qr_ref[:, c * LANES:(c + 1) * LANES] = _rope_tile(q[:, c * LANES:(c + 1) * LANES], cos, sin)
    kv = seg(OFF_KC, ZW - OFF_KC)
    part = lambda off: kv[:, off - OFF_KC:off - OFF_KC + KV_W]
    kc_ref[...] = part(OFF_KC)
    vc_ref[...] = part(OFF_VC)
    ks_ref[...] = _rope_tile(part(OFF_KS), cos, sin)
    vs_ref[...] = part(OFF_VS)
    kw_ref[...] = _rope_tile(part(OFF_KW), cos, sin)
    vw_ref[...] = part(OFF_VW)
    g_ref[...] = _sigmoid(part(OFF_GN))


def _proj(x, w, cos_t, sin_t, n_prompt_tiles, tiles_per_seq):
    rows, d = x.shape
    tm = ROW_TILE

    def tab_map(i):
        return (jnp.where(i < n_prompt_tiles, i % tiles_per_seq, tiles_per_seq), 0)

    row = lambda n: pl.BlockSpec((tm, n), lambda i: (i, 0))
    widths = (C_CONV, QW, QW) + (KV_W,) * 6 + (LANES,)
    return pl.pallas_call(
        _proj_kernel,
        grid=(rows // tm,),
        in_specs=[row(d), pl.BlockSpec((d, ZW), lambda i: (0, 0)),
                  pl.BlockSpec((tm, LANES), tab_map), pl.BlockSpec((tm, LANES), tab_map)],
        out_specs=[row(n) for n in widths],
        out_shape=[jax.ShapeDtypeStruct((rows, n), F32) for n in widths],
        compiler_params=_params("parallel"),
        name="proj_in",
    )(x, w, cos_t, sin_t)


def _conv_prompt_kernel(u_ref, w_ref, b_ref, g_ref, beta_ref, c_ref, ext_ref):
    t_len = u_ref.shape[0]
    hist = ext_ref.shape[0] - t_len
    ext_ref[0:hist, :] = jnp.zeros((hist, C_CONV), F32)
    ext_ref[hist:, :] = u_ref[...]
    first = hist - (CONV_W - 1)

    def body(i, carry):
        t0 = pl.multiple_of(i * CONV_TT, CONV_TT)
        n_win = CONV_TT + hist
        win = ext_ref[pl.ds(t0, n_win), :]
        acc = jnp.zeros((CONV_TT, C_CONV), F32) + b_ref[...]
        for r in range(SUBLANES):
            shifted = win if r == 0 else pltpu.roll(win, n_win - r, 0)
            for k in range(CONV_W):
                if (first + k) % SUBLANES == r:
                    a = first + k - r
                    acc = acc + shifted[a:a + CONV_TT] * w_ref[k:k + 1, :]
        c_ref[pl.ds(t0, CONV_TT), :] = _silu(_layer_norm(acc, g_ref[...], beta_ref[...]))
        return carry

    lax.fori_loop(0, t_len // CONV_TT, body, 0)


def _conv_prompt(u, w, b, g, beta, n_seq, t_len):
    hist = 32
    vec = pl.BlockSpec((1, C_CONV), lambda i: (0, 0))
    return pl.pallas_call(
        _conv_prompt_kernel,
        grid=(n_seq,),
        in_specs=[pl.BlockSpec((t_len, C_CONV), lambda i: (i, 0)),
                  pl.BlockSpec((CONV_W, C_CONV), lambda i: (0, 0)), vec, vec, vec],
        out_specs=pl.BlockSpec((t_len, C_CONV), lambda i: (i, 0)),
        out_shape=jax.ShapeDtypeStruct((n_seq * t_len, C_CONV), F32),
        scratch_shapes=[pltpu.VMEM((hist + t_len, C_CONV), F32)],
        compiler_params=_params("parallel"),
        name="conv_prompt",
    )(u, w, b, g, beta)


def _conv_sample_kernel(st_ref, u_ref, w_ref, b_ref, g_ref, beta_ref, c_ref):
    n_prev = st_ref.shape[0]
    n_new = u_ref.shape[0]
    for t in range(n_new):
        acc = jnp.zeros(c_ref.shape[1:], F32) + b_ref[...]
        for k in range(CONV_W):
            j = t + k
            row = st_ref[j] if j < n_prev else u_ref[j - n_prev]
            acc = acc + row * w_ref[k:k + 1, :]
        c_ref[t] = _silu(_layer_norm(acc, g_ref[...], beta_ref[...]))


def _conv_sample(state_t, u_t, w, b, g, beta):
    return pl.pallas_call(
        _conv_sample_kernel,
        out_shape=jax.ShapeDtypeStruct(u_t.shape, F32),
        compiler_params=pltpu.CompilerParams(vmem_limit_bytes=VMEM_LIMIT),
        name="conv_sample",
    )(state_t, u_t, w, b, g, beta)


def _compress_core(ch, pe_ref, w1_ref, w2_ref):
    n = ch.shape[0]
    hcat = _dot(ch.astype(BF16), w1_ref[...])
    pcat = sum(_dot(part, w1_ref[...]) for part in _split_bf16(pe_ref[...], 2))
    h0 = hcat[:, :LANES] + pcat[0:1, :LANES]
    h1 = hcat[:, LANES:] + pcat[1:2, LANES:]
    h = h0 + pltpu.roll(h1, n - 1, 0)
    return _dot(_gelu_tanh(h).astype(BF16), w2_ref[...])


def _compress_prompt_kernel(ch_ref, pe_ref, w1_ref, w2_ref, o_ref):
    o_ref[0] = _compress_core(ch_ref[...], pe_ref, w1_ref, w2_ref)


def _compress_prompt(rows, pe2, w1b, w2b, n_seq, t_len):
    n_ch = t_len // CMP_STRIDE
    kdim = CMP_STRIDE * KV_W
    ch = rows.reshape(rows.shape[0] // CMP_STRIDE, kdim)
    return pl.pallas_call(
        _compress_prompt_kernel,
        grid=(n_seq,),
        in_specs=[pl.BlockSpec((n_ch, kdim), lambda i: (i, 0)),
                  pl.BlockSpec((SUBLANES, kdim), lambda i: (0, 0)),
                  pl.BlockSpec((kdim, 2 * LANES), lambda i: (0, 0)),
                  pl.BlockSpec((LANES, LANES), lambda i: (0, 0))],
        out_specs=pl.BlockSpec((1, n_ch, LANES), lambda i: (i, 0, 0)),
        out_shape=jax.ShapeDtypeStruct((n_seq, n_ch, LANES), F32),
        compiler_params=_params("parallel"),
        name="compress_prompt",
    )(ch, pe2, w1b, w2b)


def _pages_native(cache):
    depth, n_pool, page, n_kv, hd = cache.shape
    return jnp.transpose(cache, (0, 1, 3, 4, 2)).reshape(depth, n_pool, n_kv * hd, page)


def _page_copy(cache_ref, buf_ref, sem_ref, layer, page_id, slot, p):
    page = cache_ref.shape[3]
    window = pl.ds(pl.multiple_of(p * page, page), page)
    return pltpu.make_async_copy(cache_ref.at[layer, page_id], buf_ref.at[slot, :, window], sem_ref.at[slot])


def _for_pages(n_pages, need, fn):
    def body(p, carry):
        if need is None:
            fn(p)
        else:
            @pl.when(need(p) != 0)
            def _():
                fn(p)
        return carry
    lax.fori_loop(0, n_pages, body, 0)


def _start_pages(pt_ref, cache_ref, buf_ref, sem_ref, layer, seq, slot, n_pages, need_ref=None):
    need = None if need_ref is None else (lambda p: need_ref[seq, p])
    _for_pages(n_pages, need,
               lambda p: _page_copy(cache_ref, buf_ref, sem_ref, layer, pt_ref[seq, p], slot, p).start())


def _wait_pages(pt_ref, cache_ref, buf_ref, sem_ref, layer, seq, slot, n_pages, need_ref=None):
    need = None if need_ref is None else (lambda p: need_ref[seq, p])
    _for_pages(n_pages, need,
               lambda p: _page_copy(cache_ref, buf_ref, sem_ref, layer, pt_ref[seq, p], slot, p).wait())


def _compress_sample_kernel(pt_ref, cache_ref, pe_ref, w1_ref, w2_ref, o_ref, buf_ref, rows_ref, ch_ref, sem_ref,
                            *, layer):
    b = pl.program_id(0)
    nb = pl.num_programs(0)
    n_pages = buf_ref.shape[2] // cache_ref.shape[3]
    n_ch = ch_ref.shape[0]
    slot = b % 2

    @pl.when(b == 0)
    def _():
        _start_pages(pt_ref, cache_ref, buf_ref, sem_ref, layer, 0, 0, n_pages)

    @pl.when(b + 1 < nb)
    def _():
        _start_pages(pt_ref, cache_ref, buf_ref, sem_ref, layer, b + 1, 1 - slot, n_pages)

    _wait_pages(pt_ref, cache_ref, buf_ref, sem_ref, layer, b, slot, n_pages)

    rows_ref[...] = buf_ref[slot].T
    for s in range(CMP_STRIDE):
        ch_ref[:, s * KV_W:(s + 1) * KV_W] = rows_ref[pl.ds(s, n_ch, stride=CMP_STRIDE), :].astype(BF16)
    o_ref[0] = _compress_core(ch_ref[...], pe_ref, w1_ref, w2_ref)


def _compress_sample(page_table, cache_n, pe2, w1b, w2b, layer):
    page = cache_n.shape[3]
    kdim = CMP_STRIDE * KV_W
    n_seq, n_pages = page_table.shape
    n_ch = n_pages * page // CMP_STRIDE
    grid_spec = pltpu.PrefetchScalarGridSpec(
        num_scalar_prefetch=1,
        grid=(n_seq,),
        in_specs=[pl.BlockSpec(memory_space=pl.ANY),
                  pl.BlockSpec((SUBLANES, kdim), lambda i, pt: (0, 0)),
                  pl.BlockSpec((kdim, 2 * LANES), lambda i, pt: (0, 0)),
                  pl.BlockSpec((LANES, LANES), lambda i, pt: (0, 0))],
        out_specs=pl.BlockSpec((1, n_ch, LANES), lambda i, pt: (i, 0, 0)),
        scratch_shapes=[pltpu.VMEM((2, KV_W, n_pages * page), F32), pltpu.VMEM((n_pages * page, KV_W), F32),
                        pltpu.VMEM((n_ch, kdim), BF16), pltpu.SemaphoreType.DMA((2,))],
    )
    return pl.pallas_call(
        functools.partial(_compress_sample_kernel, layer=layer),
        grid_spec=grid_spec,
        out_shape=jax.ShapeDtypeStruct((n_seq, n_ch, LANES), F32),
        compiler_params=_params("arbitrary"),
        name="compress_sample",
    )(page_table, cache_n, pe2, w1b, w2b)


def _select_mask(score_t, qpos_row, n_slc):
    p_rows = score_t.shape[0]
    j = lax.broadcasted_iota(jnp.int32, score_t.shape, 0)
    cur = qpos_row // SLC_BLK
    forced = (j == 0) | (j == cur) | (j == cur - 1)
    sc = jnp.where(forced, FORCE_SCORE, jnp.where(j <= cur, score_t, -1.0))
    sc = jnp.where(j < n_slc, sc, -2.0)
    sel = jnp.zeros(score_t.shape, F32)
    for _ in range(min(N_SEL, n_slc)):
        mx = jnp.max(sc, axis=0, keepdims=True)
        idx = jnp.min(jnp.where(sc == mx, j, p_rows), axis=0, keepdims=True)
        hit = j == idx
        sel = jnp.where(hit, 1.0, sel)
        sc = jnp.where(hit, -3.0, sc)
    return sel


def _cmp_probs(qh, kg, mask, any_row):
    s = _dot_nt(qh.astype(BF16), kg) * SCALE
    s = jnp.where(mask, s, MASK_VALUE)
    e = jnp.exp(s - jnp.max(s, axis=-1, keepdims=True))
    return e / jnp.sum(e, axis=-1, keepdims=True) * any_row


def _cmp_sel_prompt_kernel(q_ref, kc_ref, vc_ref, g_ref, mt_ref, o_ref, sel_ref, *, n_cmp, n_slc):
    tq = q_ref.shape[0]
    n_pad = kc_ref.shape[1]
    q0 = pl.program_id(1) * tq
    qpos_col = q0 + lax.broadcasted_iota(jnp.int32, (tq, 1), 0)
    qpos_row = q0 + lax.broadcasted_iota(jnp.int32, (1, tq), 1)
    n_idx = lax.broadcasted_iota(jnp.int32, (tq, n_pad), 1)
    mask = (n_idx * CMP_STRIDE + (CMP_LEN - 1) <= qpos_col) & (n_idx < n_cmp)
    any_row = (qpos_col >= CMP_LEN - 1).astype(F32)
    gates = g_ref[...]
    q = q_ref[...]
    kb = kc_ref[0].astype(BF16)
    vb = vc_ref[0].astype(BF16)
    all_causal = (q0 + tq - 1) // SLC_BLK < min(N_SEL, n_slc)
    blk = lax.broadcasted_iota(jnp.int32, (tq, sel_ref.shape[2]), 1)
    heads = [None] * N_HEADS
    for g in range(N_KV):
        s = _dot_nt(_stack_heads(q, g), kb)
        probs = []
        for h in range(HG):
            sh = jnp.where(mask, s[h * tq:(h + 1) * tq], MASK_VALUE)
            e = jnp.exp(sh - jnp.max(sh, axis=-1, keepdims=True))
            probs.append(e / jnp.sum(e, axis=-1, keepdims=True) * any_row)
        o = _dot(jnp.concatenate([p.astype(BF16) for p in probs], axis=0), vb)
        for h in range(HG):
            hh = g * HG + h
            heads[hh] = o[h * tq:(h + 1) * tq] * gates[:, 3 * hh:3 * hh + 1]
        psum = sum(probs[1:], probs[0])

        @pl.when(all_causal)
        def _():
            sel_ref[g] = (blk <= qpos_col // SLC_BLK).astype(F32)

        @pl.when(jnp.logical_not(all_causal))
        def _():
            score_t = sum(_dot_nt(mt_ref[...], part) for part in _split_bf16(psum, 3))
            sel_t = _select_mask(score_t, qpos_row, n_slc)
            unused = jnp.zeros((sel_ref.shape[2] - sel_t.shape[0], tq), F32)
            sel_ref[g] = jnp.concatenate([sel_t, unused], axis=0).T

    for c, col in enumerate(_assemble_heads(heads)):
        o_ref[:, c * LANES:(c + 1) * LANES] = col


def _cmp_sel_prompt(q, kcmp, vcmp, gates, mt, n_seq, t_len, n_cmp, n_slc):
    tq = ATT_TQ
    nq = t_len // tq
    n_pad = kcmp.shape[1]
    rows = n_seq * t_len
    kv = pl.BlockSpec((1, n_pad, LANES), lambda b, i: (b, 0, 0))
    return pl.pallas_call(
        functools.partial(_cmp_sel_prompt_kernel, n_cmp=n_cmp, n_slc=n_slc),
        grid=(n_seq, nq),
        in_specs=[pl.BlockSpec((tq, QW), lambda b, i: (b * nq + i, 0)), kv, kv,
                  pl.BlockSpec((tq, LANES), lambda b, i: (b * nq + i, 0)),
                  pl.BlockSpec(mt.shape, lambda b, i: (0, 0))],
        out_specs=[pl.BlockSpec((tq, QW), lambda b, i: (b * nq + i, 0)),
                   pl.BlockSpec((N_KV, tq, SEL_ROWS_PROMPT), lambda b, i: (0, b * nq + i, 0))],
        out_shape=[jax.ShapeDtypeStruct((rows, QW), F32),
                   jax.ShapeDtypeStruct((N_KV, rows, SEL_ROWS_PROMPT), F32)],
        compiler_params=_params("parallel", "parallel"),
        name="cmp_select_prompt",
    )(q, kcmp, vcmp, gates, mt)


def _slc_prompt_kernel(qr_ref, ks_ref, vs_ref, sel_ref, e_ref, g_ref, o_ref, qs_ref, m_ref, acc_ref):
    tq = qr_ref.shape[0]
    tk = e_ref.shape[2]
    q0 = pl.program_id(1) * tq
    qpos = q0 + lax.broadcasted_iota(jnp.int32, (tq, 1), 0)
    q = qr_ref[...]
    for g in range(N_KV):
        qs_ref[g] = _stack_heads(q, g)
    m_ref[...] = jnp.full(m_ref.shape, MASK_VALUE, F32)
    acc_ref[...] = jnp.zeros(acc_ref.shape, F32)

    def body(c, carry):
        k0 = pl.multiple_of(c * tk, tk)
        causal = (k0 + lax.broadcasted_iota(jnp.int32, (tq, tk), 1)) <= qpos
        kb = ks_ref[pl.ds(k0, tk), :].astype(BF16)
        v = vs_ref[pl.ds(k0, tk), :]
        for g in range(N_KV):
            chosen = _dot(sel_ref[g].astype(BF16), e_ref[c])
            mask = causal & (chosen > 0.5)
            s = _dot_nt(qs_ref[g], kb)
            probs, alphas = [], []
            for h in range(HG):
                hh = g * HG + h
                sh = jnp.where(mask, s[h * tq:(h + 1) * tq], MASK_VALUE)
                m_old = m_ref[hh]
                m_new = jnp.maximum(m_old, jnp.max(sh, axis=-1, keepdims=True))
                m_ref[hh] = m_new
                probs.append(jnp.exp(sh - jnp.concatenate([m_new] * (tk // LANES), axis=1)).astype(BF16))
                alphas.append(jnp.exp(m_old - m_new))
            pv = _dot(jnp.concatenate(probs, axis=0), _v_ext(v, g))
            for h in range(HG):
                hh = g * HG + h
                acc_ref[hh] = alphas[h] * acc_ref[hh] + pv[h * tq:(h + 1) * tq]
        return carry

    lax.fori_loop(0, (q0 + tq + tk - 1) // tk, body, 0)
    gates = g_ref[...]
    heads = [_normalise(acc_ref[hh]) * gates[:, 3 * hh + 1:3 * hh + 2] for hh in range(N_HEADS)]
    for c, col in enumerate(_assemble_heads(heads)):
        o_ref[:, c * LANES:(c + 1) * LANES] = col


def _slc_prompt(qr, ks, vs, sel, expand, gates, n_seq, t_len):
    tq = ATT_TQ
    nq = t_len // tq
    rows = n_seq * t_len
    kv = pl.BlockSpec((t_len, KV_W), lambda b, i: (b, 0))
    return pl.pallas_call(
        _slc_prompt_kernel,
        grid=(n_seq, nq),
        in_specs=[pl.BlockSpec((tq, QW), lambda b, i: (b * nq + i, 0)), kv, kv,
                  pl.BlockSpec((N_KV, tq, SEL_ROWS_PROMPT), lambda b, i: (0, b * nq + i, 0)),
                  pl.BlockSpec(expand.shape, lambda b, i: (0, 0, 0)),
                  pl.BlockSpec((tq, LANES), lambda b, i: (b * nq + i, 0))],
        out_specs=pl.BlockSpec((tq, QW), lambda b, i: (b * nq + i, 0)),
        out_shape=jax.ShapeDtypeStruct((rows, QW), F32),
        scratch_shapes=[pltpu.VMEM((N_KV, HG * tq, LANES), BF16),
                        pltpu.VMEM((N_HEADS, tq, LANES), F32), pltpu.VMEM((N_HEADS, tq, LANES), F32)],
        compiler_params=_params("parallel", "parallel"),
        name="slc_prompt",
    )(qr, ks, vs, sel, expand, gates)


def _win_prompt_kernel(qr_ref, kw_ref, vw_ref, g_ref, o_ref, *, nk):
    tq = qr_ref.shape[0]
    q0 = pl.program_id(1) * tq
    start = pl.multiple_of(jnp.maximum(q0 + tq - nk, 0), tq)
    qpos = q0 + lax.broadcasted_iota(jnp.int32, (tq, 1), 0)
    diff = qpos - (start + lax.broadcasted_iota(jnp.int32, (tq, nk), 1))
    mask = (diff >= 0) & (diff < WINDOW)
    gates = g_ref[...]
    q = qr_ref[...]
    kb = kw_ref[pl.ds(start, nk), :].astype(BF16)
    v = vw_ref[pl.ds(start, nk), :]
    heads = [None] * N_HEADS
    for g in range(N_KV):
        s = _dot_nt(_stack_heads(q, g), kb)
        probs = []
        for h in range(HG):
            sh = jnp.where(mask, s[h * tq:(h + 1) * tq], MASK_VALUE)
            probs.append(jnp.exp(sh - jnp.max(sh, axis=-1, keepdims=True)).astype(BF16))
        pv = _dot(jnp.concatenate(probs, axis=0), _v_ext(v, g))
        for h in range(HG):
            hh = g * HG + h
            heads[hh] = _normalise(pv[h * tq:(h + 1) * tq]) * gates[:, 3 * hh + 2:3 * hh + 3]
    for c, col in enumerate(_assemble_heads(heads)):
        o_ref[:, c * LANES:(c + 1) * LANES] = col


def _win_prompt(qr, kw, vw, gates, n_seq, t_len):
    tq = ATT_TQ
    nq = t_len // tq
    nk = min(WINDOW + tq, t_len)
    kv = pl.BlockSpec((t_len, KV_W), lambda b, i: (b, 0))
    return pl.pallas_call(
        functools.partial(_win_prompt_kernel, nk=nk),
        grid=(n_seq, nq),
        in_specs=[pl.BlockSpec((tq, QW), lambda b, i: (b * nq + i, 0)), kv, kv,
                  pl.BlockSpec((tq, LANES), lambda b, i: (b * nq + i, 0))],
        out_specs=pl.BlockSpec((tq, QW), lambda b, i: (b * nq + i, 0)),
        out_shape=jax.ShapeDtypeStruct((n_seq * t_len, QW), F32),
        compiler_params=_params("parallel", "parallel"),
        name="win_prompt",
    )(qr, kw, vw, gates)


def _cmp_win_sample_kernel(q_ref, qr_ref, kc_ref, vc_ref, wk_ref, wv_ref, nk_ref, nv_ref, g_ref,
                           o_ref, ps_ref, *, past, n_cmp, n_new):
    nq = q_ref.shape[1]
    n_pad = kc_ref.shape[1]
    wbuf = wk_ref.shape[1]
    qpos_col = past + lax.broadcasted_iota(jnp.int32, (nq, 1), 0)
    n_idx = lax.broadcasted_iota(jnp.int32, (nq, n_pad), 1)
    mask_c = (n_idx * CMP_STRIDE + (CMP_LEN - 1) <= qpos_col) & (n_idx < n_cmp)
    any_row = (qpos_col >= CMP_LEN - 1).astype(F32)
    t_col = lax.broadcasted_iota(jnp.int32, (nq, 1), 0)
    diff_buf = t_col + wbuf - lax.broadcasted_iota(jnp.int32, (nq, wbuf), 1)
    mask_buf = (diff_buf >= 0) & (diff_buf < WINDOW)
    r_idx = lax.broadcasted_iota(jnp.int32, (nq, nq), 1)
    mask_new = (t_col - r_idx >= 0) & (t_col - r_idx < WINDOW) & (r_idx < n_new)
    gates = g_ref[0]
    q = q_ref[0]
    qr = qr_ref[0]
    kcb = kc_ref[0].astype(BF16)
    vcb = vc_ref[0].astype(BF16)
    wkb = wk_ref[0].astype(BF16)
    nkb = nk_ref[0].astype(BF16)
    heads = [None] * N_HEADS
    rows = lambda a, h: a[h * nq:(h + 1) * nq]
    stack = lambda parts: jnp.concatenate(parts, axis=0).astype(BF16)
    for g in range(N_KV):
        sc = _dot_nt(_stack_heads(q, g), kcb)
        qs = _stack_heads(qr, g)
        s1 = _dot_nt(qs, wkb)
        s2 = _dot_nt(qs, nkb)
        probs, e1s, e2s = [], [], []
        for h in range(HG):
            sh = jnp.where(mask_c, rows(sc, h), MASK_VALUE)
            e = jnp.exp(sh - jnp.max(sh, axis=-1, keepdims=True))
            probs.append(e / jnp.sum(e, axis=-1, keepdims=True) * any_row)
            a1 = jnp.where(mask_buf, rows(s1, h), MASK_VALUE)
            a2 = jnp.where(mask_new, rows(s2, h), MASK_VALUE)
            m = jnp.maximum(jnp.max(a1, axis=-1, keepdims=True), jnp.max(a2, axis=-1, keepdims=True))
            e1s.append(jnp.exp(a1 - m))
            e2s.append(jnp.exp(a2 - m))
        o_cmp = _dot(stack(probs), vcb)
        o_win = _normalise(_dot(stack(e1s), _v_ext(wv_ref[0], g)) + _dot(stack(e2s), _v_ext(nv_ref[0], g)))
        for h in range(HG):
            hh = g * HG + h
            heads[hh] = (rows(o_cmp, h) * gates[:, 3 * hh:3 * hh + 1]
                         + rows(o_win, h) * gates[:, 3 * hh + 2:3 * hh + 3])
        ps_ref[0, g] = sum(probs[1:], probs[0])
    for c, col in enumerate(_assemble_heads(heads)):
        o_ref[0, :, c * LANES:(c + 1) * LANES] = col


def _cmp_win_sample(q, qr, kcmp, vcmp, wk, wv, nk, nv, gates, past, n_cmp, n_new):
    n_seq, nq, _ = q.shape
    n_pad = kcmp.shape[1]
    per_seq = lambda a: pl.BlockSpec((1,) + a.shape[1:], lambda b: (b, 0, 0))
    return pl.pallas_call(
        functools.partial(_cmp_win_sample_kernel, past=past, n_cmp=n_cmp, n_new=n_new),
        grid=(n_seq,),
        in_specs=[per_seq(a) for a in (q, qr, kcmp, vcmp, wk, wv, nk, nv, gates)],
        out_specs=[pl.BlockSpec((1, nq, QW), lambda b: (b, 0, 0)),
                   pl.BlockSpec((1, N_KV, nq, n_pad), lambda b: (b, 0, 0, 0))],
        out_shape=[jax.ShapeDtypeStruct((n_seq, nq, QW), F32),
                   jax.ShapeDtypeStruct((n_seq, N_KV, nq, n_pad), F32)],
        compiler_params=_params("parallel"),
        name="cmp_win_sample",
    )(q, qr, kcmp, vcmp, wk, wv, nk, nv, gates)


def _select_sample_kernel(ps_ref, mt_ref, e_ref, selx_ref, *, past, n_slc, n_new):
    nq_all = ps_ref.shape[1]
    chunk = 8 * LANES
    tok_row = lax.broadcasted_iota(jnp.int32, (1, nq_all), 1) % DEC_PAD
    live = (lax.broadcasted_iota(jnp.int32, (nq_all, 1), 0) % DEC_PAD < n_new).astype(F32)
    for g in range(N_KV):
        score_t = sum(_dot_nt(mt_ref[...], part) for part in _split_bf16(ps_ref[g], 3))
        sel = (_select_mask(score_t, past + tok_row, n_slc).T * live).astype(BF16)
        for c in range(past // chunk):
            selx_ref[g, :, c * chunk:(c + 1) * chunk] = _dot(sel, e_ref[:, c * chunk:(c + 1) * chunk])


def _select_sample(psum, mt, expand, past, n_slc, n_new):
    assert past % (8 * LANES) == 0
    return pl.pallas_call(
        functools.partial(_select_sample_kernel, past=past, n_slc=n_slc, n_new=n_new),
        out_shape=jax.ShapeDtypeStruct((N_KV, psum.shape[1], past), F32),
        compiler_params=pltpu.CompilerParams(vmem_limit_bytes=VMEM_LIMIT),
        name="select_sample",
    )(psum, mt, expand)


def _slc_sample_kernel(pt_ref, need_ref, qr_ref, selx_ref, nk_ref, nv_ref, g_ref, part_ref, kcache_ref, vcache_ref,
                       o_ref, kbuf_ref, vbuf_ref, ksem_ref, vsem_ref, *, layer, n_new):
    b = pl.program_id(0)
    nb = pl.num_programs(0)
    n_pages = kbuf_ref.shape[2] // kcache_ref.shape[3]
    nq = qr_ref.shape[1]
    slot = b % 2

    def start(seq, sl):
        _start_pages(pt_ref, kcache_ref, kbuf_ref, ksem_ref, layer, seq, sl, n_pages, need_ref)
        _start_pages(pt_ref, vcache_ref, vbuf_ref, vsem_ref, layer, seq, sl, n_pages, need_ref)

    @pl.when(b == 0)
    def _():
        kbuf_ref[...] = jnp.zeros(kbuf_ref.shape, F32)
        vbuf_ref[...] = jnp.zeros(vbuf_ref.shape, F32)
        start(0, 0)

    @pl.when(b + 1 < nb)
    def _():
        start(b + 1, 1 - slot)

    _wait_pages(pt_ref, kcache_ref, kbuf_ref, ksem_ref, layer, b, slot, n_pages, need_ref)
    _wait_pages(pt_ref, vcache_ref, vbuf_ref, vsem_ref, layer, b, slot, n_pages, need_ref)

    q = qr_ref[0]
    qs = jnp.concatenate([_stack_heads(q, g) for g in range(N_KV)], axis=0)
    s_past = _dot(qs, kbuf_ref[slot].astype(BF16))
    s_new = _dot_nt(qs, nk_ref[0].astype(BF16))
    t_col = lax.broadcasted_iota(jnp.int32, (nq, 1), 0)
    r_idx = lax.broadcasted_iota(jnp.int32, (nq, nq), 1)
    mask_new = (r_idx <= t_col) & (r_idx < n_new)
    e_past, e_new, dens = [], [], []
    for hh in range(N_HEADS):
        rs = slice(hh * nq, (hh + 1) * nq)
        chosen = selx_ref[hh // HG] > 0.5
        sm = jnp.where(chosen, s_past[rs], MASK_VALUE)
        sn = jnp.where(mask_new, s_new[rs], MASK_VALUE)
        m = jnp.maximum(jnp.max(sm, axis=-1, keepdims=True), jnp.max(sn, axis=-1, keepdims=True))
        e = jnp.exp(sm - m)
        en = jnp.exp(sn - m)
        e_past.append(e)
        e_new.append(en)
        dens.append(jnp.sum(e, axis=-1, keepdims=True) + jnp.sum(en, axis=-1, keepdims=True))
    acc = (_dot_nt(jnp.concatenate(e_past, axis=0).astype(BF16), vbuf_ref[slot].astype(BF16))
           + _dot(jnp.concatenate(e_new, axis=0).astype(BF16), nv_ref[0].astype(BF16)))
    o = acc / jnp.concatenate(dens, axis=0)
    gates = g_ref[0]
    heads = [o[hh * nq:(hh + 1) * nq] * gates[:, 3 * hh + 1:3 * hh + 2] for hh in range(N_HEADS)]
    for c, col in enumerate(_assemble_heads(heads)):
        o_ref[0, :, c * LANES:(c + 1) * LANES] = part_ref[0, :, c * LANES:(c + 1) * LANES] + col


def _slc_sample(page_table, need, qr, selx, nk, nv, gates, part, kcache_n, vcache_n, layer, n_new):
    n_seq, nq, _ = qr.shape
    n_pages = page_table.shape[1]
    page = kcache_n.shape[3]
    per_seq = lambda a: pl.BlockSpec((1,) + a.shape[1:], lambda b, pt, nd: (b, 0, 0))
    grid_spec = pltpu.PrefetchScalarGridSpec(
        num_scalar_prefetch=2,
        grid=(n_seq,),
        in_specs=[per_seq(qr), pl.BlockSpec((N_KV, nq, n_pages * page), lambda b, pt, nd: (0, b, 0))]
                 + [per_seq(a) for a in (nk, nv, gates, part)]
                 + [pl.BlockSpec(memory_space=pl.ANY), pl.BlockSpec(memory_space=pl.ANY)],
        out_specs=pl.BlockSpec((1, nq, QW), lambda b, pt, nd: (b, 0, 0)),
        scratch_shapes=[pltpu.VMEM((2, KV_W, n_pages * page), F32), pltpu.VMEM((2, KV_W, n_pages * page), F32),
                        pltpu.SemaphoreType.DMA((2,)), pltpu.SemaphoreType.DMA((2,))],
    )
    return pl.pallas_call(
        functools.partial(_slc_sample_kernel, layer=layer, n_new=n_new),
        grid_spec=grid_spec,
        out_shape=jax.ShapeDtypeStruct((n_seq, nq, QW), F32),
        compiler_params=_params("arbitrary"),
        name="slc_sample",
    )(page_table, need, qr, selx, nk, nv, gates, part, kcache_n, vcache_n)


def _merge_kernel(x_ref, cp_ref, cs_ref, op1_ref, op2_ref, op3_ref, os_ref,
                  wga_ref, wgb_ref, wco_ref, wao_ref, wo_ref, g_ref, b_ref, y_ref, *, n_prompt_tiles, alpha):
    is_prompt = pl.program_id(0) < n_prompt_tiles
    x = x_ref[...]
    xb = x.astype(BF16)
    c = jnp.where(is_prompt, cp_ref[...], cs_ref[...])
    o = jnp.where(is_prompt, op1_ref[...] + op2_ref[...] + op3_ref[...], os_ref[...])
    mixed = _sigmoid(_dot(xb, wga_ref[...])) * _dot(c.astype(BF16), wco_ref[...])
    mixed = mixed + _sigmoid(_dot(xb, wgb_ref[...])) * _dot(o.astype(BF16), wao_ref[...])
    y = _dot(mixed.astype(BF16), wo_ref[...])
    y_ref[...] = _layer_norm(alpha * x + y, g_ref[...], b_ref[...])


def _merge(x, c_p, c_s, o_cmp, o_slc, o_win, o_s, wga, wgb, wco, wao, wo, g, b, alpha):
    rows, d = x.shape
    tm = MERGE_TILE
    n_pt = c_p.shape[0] // tm
    prm = lambda n: pl.BlockSpec((tm, n), lambda i: (jnp.minimum(i, n_pt - 1), 0))
    smp = lambda n: pl.BlockSpec((tm, n), lambda i: (jnp.maximum(i - n_pt, 0), 0))
    full = lambda a: pl.BlockSpec(a.shape, lambda i: (0, 0))
    return pl.pallas_call(
        functools.partial(_merge_kernel, n_prompt_tiles=n_pt, alpha=alpha),
        grid=(rows // tm,),
        in_specs=[pl.BlockSpec((tm, d), lambda i: (i, 0)), prm(C_CONV), smp(C_CONV),
                  prm(QW), prm(QW), prm(QW), smp(QW),
                  full(wga), full(wgb), full(wco), full(wao), full(wo), full(g), full(b)],
        out_specs=pl.BlockSpec((tm, d), lambda i: (i, 0)),
        out_shape=jax.ShapeDtypeStruct((rows, d), F32),
        compiler_params=_params("parallel"),
        name="merge",
    )(x, c_p, c_s, o_cmp, o_slc, o_win, o_s, wga, wgb, wco, wao, wo, g, b)


def _ffn_kernel(x_ref, wg_ref, wu_ref, wd_ref, g_ref, b_ref, y_ref, xb_ref, acc_ref, *, alpha):
    j = pl.program_id(1)

    @pl.when(j == 0)
    def _():
        xb_ref[...] = x_ref[...].astype(BF16)
        acc_ref[...] = jnp.zeros(acc_ref.shape, F32)

    xb = xb_ref[...]
    h = _silu(_dot(xb, wg_ref[...])) * _dot(xb, wu_ref[...])
    acc_ref[...] += _dot(h.astype(BF16), wd_ref[...])

    @pl.when(j == pl.num_programs(1) - 1)
    def _():
        y_ref[...] = _layer_norm(alpha * x_ref[...] + acc_ref[...], g_ref[...], b_ref[...])


def _ffn(x, wg, wu, wd, g, b, alpha):
    rows, d = x.shape
    d_ff = wg.shape[1]
    tm = ROW_TILE
    tf = d_ff // 2 if d_ff % (2 * LANES) == 0 else LANES
    vec = pl.BlockSpec((1, d), lambda i, j: (0, 0))
    return pl.pallas_call(
        functools.partial(_ffn_kernel, alpha=alpha),
        grid=(rows // tm, d_ff // tf),
        in_specs=[pl.BlockSpec((tm, d), lambda i, j: (i, 0)),
                  pl.BlockSpec((d, tf), lambda i, j: (0, j)), pl.BlockSpec((d, tf), lambda i, j: (0, j)),
                  pl.BlockSpec((tf, d), lambda i, j: (j, 0)), vec, vec],
        out_specs=pl.BlockSpec((tm, d), lambda i, j: (i, 0)),
        out_shape=jax.ShapeDtypeStruct((rows, d), F32),
        scratch_shapes=[pltpu.VMEM((tm, d), BF16), pltpu.VMEM((tm, d), F32)],
        compiler_params=_params("parallel", "arbitrary"),
        name="ffn",
    )(x, wg, wu, wd, g, b)


def _rope_tables(pos):
    half = HEAD_DIM // 2
    inv = ROPE_THETA ** (-jnp.arange(half, dtype=F32) / half)
    ang = pos.astype(F32)[:, None] * inv[None, :]
    cos, sin = jnp.cos(ang), jnp.sin(ang)
    reps = LANES // HEAD_DIM
    return (jnp.tile(jnp.concatenate([cos, cos], -1), (1, reps)),
            jnp.tile(jnp.concatenate([-sin, sin], -1), (1, reps)))


def _overlap_t(n_cmp, n_slc, rows, cols):
    i = np.arange(n_cmp)[:, None]
    j = np.arange(n_slc)[None, :]
    ov = np.minimum(i * CMP_STRIDE + CMP_LEN, (j + 1) * SLC_BLK) - np.maximum(i * CMP_STRIDE, j * SLC_BLK)
    m = np.zeros((rows, cols), np.float32)
    m[:n_slc, :n_cmp] = (np.clip(ov, 0, None) / CMP_LEN).T
    return jnp.asarray(m, dtype=BF16)


def _expand_table(rows, n_keys):
    e = (np.arange(n_keys)[None, :] // SLC_BLK) == np.arange(rows)[:, None]
    return jnp.asarray(e.astype(np.float32), dtype=BF16)


def _compress_weights(pe, w1, w2):
    r = CMP_LEN // CMP_STRIDE
    eye = jnp.eye(N_KV, dtype=F32)
    w1_r = w1.reshape(r, CMP_STRIDE, HEAD_DIM, CMP_HID)
    w1b = jnp.einsum('msde,gh->sgdmhe', w1_r, eye).reshape(CMP_STRIDE * KV_W, r * N_KV * CMP_HID).astype(BF16)
    w2b = jnp.einsum('ed,gh->gehd', w2, eye).reshape(N_KV * CMP_HID, KV_W).astype(BF16)
    pe_r = pe.reshape(r, CMP_STRIDE, 1, HEAD_DIM)
    pe2 = jnp.broadcast_to(pe_r, (r, CMP_STRIDE, N_KV, HEAD_DIM)).reshape(r, CMP_STRIDE * KV_W)
    pe2 = jnp.pad(pe2, ((0, SUBLANES - r), (0, 0)))
    return pe2, w1b, w2b


def _pad_rows(a, n):
    return jnp.pad(a, ((0, 0), (0, n - a.shape[1]), (0, 0)))


def kernel(x_prompt, x_sample, cache_cmp_k, cache_cmp_v, cache_slc_k, cache_slc_v, state_win_k, state_win_v, state_conv, page_table, w_in, cmp_pe_k, cmp_w1_k, cmp_w2_k, cmp_pe_v, cmp_w1_v, cmp_w2_v, conv_w, conv_b, conv_ln_g, conv_ln_b, w_conv_out, w_attn_out, w_o, ln1_g, ln1_b, w_gate, w_up, w_down, ln2_g, ln2_b):
    n_seq, t_len, d = x_prompt.shape
    db, dt, _ = x_sample.shape
    depth = w_in.shape[0]
    page = cache_cmp_k.shape[2]
    past = page_table.shape[1] * page
    wbuf = state_win_k.shape[2]
    alpha = (2.0 * depth) ** 0.25
    np_rows = n_seq * t_len
    ns_rows = db * dt
    assert t_len % ROW_TILE == 0 and ns_rows <= ROW_TILE and dt <= DEC_PAD
    assert QW == 4 * LANES and past % CMP_STRIDE == 0 and past >= CMP_LEN
    n_prompt_tiles = np_rows // ROW_TILE
    tiles_per_seq = t_len // ROW_TILE

    x = jnp.concatenate([x_prompt.reshape(np_rows, d), x_sample.reshape(ns_rows, d),
                         jnp.zeros((ROW_TILE - ns_rows, d), F32)], axis=0)
    pos = jnp.concatenate([jnp.arange(t_len), past + (jnp.arange(ROW_TILE) % dt)])
    cos_t, sin_t = _rope_tables(pos)

    n_cmp_p = (t_len - CMP_LEN) // CMP_STRIDE + 1
    n_slc_p = -(-t_len // SLC_BLK)
    assert n_slc_p <= SEL_ROWS_PROMPT and t_len // CMP_STRIDE <= LANES
    mt_p = _overlap_t(n_cmp_p, n_slc_p, -(-n_slc_p // SUBLANES) * SUBLANES, t_len // CMP_STRIDE)
    exp_p = _expand_table(SEL_ROWS_PROMPT, t_len).reshape(SEL_ROWS_PROMPT, t_len // SLC_TK, SLC_TK).transpose(1, 0, 2)
    n_all = past + dt
    n_cmp_s = (n_all - CMP_LEN) // CMP_STRIDE + 1
    n_slc_s = -(-n_all // SLC_BLK)
    n_ch_s = past // CMP_STRIDE
    assert n_cmp_s + 1 <= n_ch_s and n_slc_s <= SEL_ROWS_SAMPLE and (n_slc_s - 1) * SLC_BLK == past
    mt_s = _overlap_t(n_cmp_s, n_slc_s, SEL_ROWS_SAMPLE, n_ch_s)
    exp_s = _expand_table(SEL_ROWS_SAMPLE, past)
    n_pages = page_table.shape[1]
    assert page % SLC_BLK == 0 and KV_W == LANES
    cmp_k_n, cmp_v_n, slc_k_n, slc_v_n = (_pages_native(c) for c in (cache_cmp_k, cache_cmp_v, cache_slc_k, cache_slc_v))

    bf = lambda a: a.astype(BF16)
    outs_p = [[] for _ in range(7)]
    outs_s = [[] for _ in range(7)]
    for l in range(depth):
        w_z = bf(w_in[l, :, :ZW])
        w_ga = bf(w_in[l, :, OFF_GA:OFF_GA + d])
        w_gb = bf(w_in[l, :, OFF_GA + d:OFF_GA + 2 * d])
        u, q, qr, kc, vc, ks, vs, kw, vw, gates = _proj(x, w_z, cos_t, sin_t, n_prompt_tiles, tiles_per_seq)

        smp = lambda a: a[np_rows:np_rows + ns_rows].reshape(db, dt, a.shape[-1])
        smp8 = lambda a: _pad_rows(smp(a), DEC_PAD)
        row2 = lambda v: v.reshape(1, -1)

        cw, cb, cg, cbeta = conv_w[l], row2(conv_b[l]), row2(conv_ln_g[l]), row2(conv_ln_b[l])
        c_p = _conv_prompt(u, cw, cb, cg, cbeta, n_seq, t_len)
        u_s = smp(u)
        c_s = _conv_sample(state_conv[l].transpose(1, 0, 2), u_s.transpose(1, 0, 2), cw, cb, cg, cbeta)
        c_s = jnp.pad(c_s.transpose(1, 0, 2).reshape(ns_rows, C_CONV), ((0, ROW_TILE - ns_rows), (0, 0)))

        pe2k, w1k, w2k = _compress_weights(cmp_pe_k[l], cmp_w1_k[l], cmp_w2_k[l])
        pe2v, w1v, w2v = _compress_weights(cmp_pe_v[l], cmp_w1_v[l], cmp_w2_v[l])
        kcmp_p = _compress_prompt(kc, pe2k, w1k, w2k, n_seq, t_len)
        vcmp_p = _compress_prompt(vc, pe2v, w1v, w2v, n_seq, t_len)
        kcmp_s = _compress_sample(page_table, cmp_k_n, pe2k, w1k, w2k, l)
        vcmp_s = _compress_sample(page_table, cmp_v_n, pe2v, w1v, w2v, l)

        o_cmp, sel_p = _cmp_sel_prompt(q, kcmp_p, vcmp_p, gates, mt_p, n_seq, t_len, n_cmp_p, n_slc_p)
        o_slc = _slc_prompt(qr, ks, vs, sel_p, exp_p, gates, n_seq, t_len)
        o_win = _win_prompt(qr, kw, vw, gates, n_seq, t_len)

        ks_s, vs_s, kw_s, vw_s = smp(ks), smp(vs), smp(kw), smp(vw)
        q8, qr8, g8 = smp8(q), smp8(qr), smp8(gates)
        wk = state_win_k[l].reshape(db, wbuf, KV_W)
        wv = state_win_v[l].reshape(db, wbuf, KV_W)
        part, psum_s = _cmp_win_sample(q8, qr8, kcmp_s, vcmp_s, wk, wv, _pad_rows(kw_s, DEC_PAD),
                                       _pad_rows(vw_s, DEC_PAD), g8, past, n_cmp_s, dt)
        psum_s = psum_s.transpose(1, 0, 2, 3).reshape(N_KV, db * DEC_PAD, n_ch_s)
        selx = _select_sample(psum_s, mt_s, exp_s, past, n_slc_s, dt)
        need = (selx.reshape(N_KV, db, DEC_PAD, n_pages, page).max(axis=(0, 2, 4)) > 0.5).astype(jnp.int32)
        o_s8 = _slc_sample(page_table, need, qr8, selx, _pad_rows(ks_s, DEC_PAD), _pad_rows(vs_s, DEC_PAD), g8, part,
                           slc_k_n, slc_v_n, l, dt)
        o_s = jnp.pad(o_s8[:, :dt].reshape(ns_rows, QW), ((0, ROW_TILE - ns_rows), (0, 0)))

        x = _merge(x, c_p, c_s, o_cmp, o_slc, o_win, o_s, w_ga, w_gb, bf(w_conv_out[l]), bf(w_attn_out[l]),
                   bf(w_o[l]), row2(ln1_g[l]), row2(ln1_b[l]), alpha)
        x = _ffn(x, bf(w_gate[l]), bf(w_up[l]), bf(w_down[l]), row2(ln2_g[l]), row2(ln2_b[l]), alpha)

        prm = lambda a: a[:np_rows].reshape(n_seq, t_len, N_KV, HEAD_DIM)
        kv4 = lambda a: a.reshape(db, -1, N_KV, HEAD_DIM)
        n_keep = min(WINDOW, t_len)
        new_p = (u[:np_rows].reshape(n_seq, t_len, C_CONV)[:, t_len - (CONV_W - 1):], prm(kc), prm(vc), prm(ks), prm(vs),
                 prm(kw)[:, t_len - n_keep:], prm(vw)[:, t_len - n_keep:])
        new_s = (jnp.concatenate([state_conv[l], u_s], axis=1)[:, -(CONV_W - 1):],
                 kv4(smp(kc)), kv4(smp(vc)), kv4(ks_s), kv4(vs_s),
                 jnp.concatenate([state_win_k[l], kv4(kw_s)], axis=1)[:, dt:],
                 jnp.concatenate([state_win_v[l], kv4(vw_s)], axis=1)[:, dt:])
        for i in range(7):
            outs_p[i].append(new_p[i])
            outs_s[i].append(new_s[i])

    stk = lambda rows: jnp.stack(rows, axis=0)
    order = (1, 2, 3, 4, 5, 6, 0)
    return ((x[:np_rows].reshape(n_seq, t_len, d), x[np_rows:np_rows + ns_rows].reshape(db, dt, d))
            + tuple(stk(outs_p[i]) for i in order) + tuple(stk(outs_s[i]) for i in order))
```

```python
import functools
import math

import numpy as np
import jax
import jax.numpy as jnp
from jax import lax
from jax.experimental import pallas as pl
from jax.experimental.pallas import tpu as pltpu

F32 = jnp.float32
BF16 = jnp.bfloat16

C_CONV = 512
CONV_W = 31
N_HEADS = 8
N_KV = 2
HG = N_HEADS // N_KV
HEAD_DIM = 64
CMP_LEN = 32
CMP_STRIDE = 16
CMP_HID = 64
SLC_BLK = 64
N_SEL = 16
WINDOW = 512
ROPE_THETA = 10000.0
LN_EPS = 1e-5
MASK_VALUE = -1e30
FORCE_SCORE = 1e4
KV_W = N_KV * HEAD_DIM
QW = N_HEADS * HEAD_DIM
SCALE = HEAD_DIM ** -0.5
LOG2E = math.log2(math.e)

LANES = 128
SUBLANES = 8
VMEM_LIMIT = 56 * 1024 * 1024

OFF_UA = 0
OFF_UG = C_CONV
OFF_Q = 2 * C_CONV
OFF_KC = OFF_Q + QW
OFF_VC = OFF_KC + KV_W
OFF_KS = OFF_VC + KV_W
OFF_VS = OFF_KS + KV_W
OFF_KW = OFF_VS + KV_W
OFF_VW = OFF_KW + KV_W
OFF_GN = OFF_VW + KV_W
OFF_GA = OFF_GN + 3 * N_HEADS
ZW = OFF_GN + LANES

ROW_TILE = 512
MERGE_TILE = 256
DEC_PAD = 8
ATT_TQ = 256
SLC_TK = 512
CONV_TT = 64
SEL_ROWS_PROMPT = 128
SEL_ROWS_SAMPLE = 256


def _params(*sem):
    return pltpu.CompilerParams(dimension_semantics=sem, vmem_limit_bytes=VMEM_LIMIT)


def _sigmoid(x):
    return 1.0 / (1.0 + jnp.exp(-x))


def _silu(x):
    return x * _sigmoid(x)


def _gelu_tanh(x):
    return 0.5 * x * (1.0 + jnp.tanh(math.sqrt(2.0 / math.pi) * (x + 0.044715 * (x * x * x))))


def _layer_norm(y, g, b):
    mu = jnp.mean(y, axis=-1, keepdims=True)
    d = y - mu
    var = jnp.mean(d * d, axis=-1, keepdims=True)
    return d * lax.rsqrt(var + LN_EPS) * g + b


def _dot(a, b):
    return jnp.dot(a, b, preferred_element_type=F32)


def _dot_nt(a, b):
    return lax.dot_general(a, b, (((1,), (1,)), ((), ())), preferred_element_type=F32)


def _split_bf16(a, terms):
    parts, rest = [], a
    for _ in range(terms):
        part = rest.astype(BF16)
        parts.append(part)
        rest = rest - part.astype(F32)
    return parts


def _rope_tile(x, cos, sin_signed):
    lane = lax.broadcasted_iota(jnp.int32, x.shape, 1)
    first = (lane % HEAD_DIM) < (HEAD_DIM // 2)
    rot = jnp.where(first, pltpu.roll(x, LANES - HEAD_DIM // 2, 1), pltpu.roll(x, HEAD_DIM // 2, 1))
    return x * cos + rot * sin_signed


def _stack_heads(q, g):
    tq = q.shape[0]
    in_half = (lax.broadcasted_iota(jnp.int32, (tq, LANES), 1) // HEAD_DIM) == g
    slabs = []
    for h in range(HG):
        hh = g * HG + h
        col = q[:, (hh // 2) * LANES:(hh // 2 + 1) * LANES]
        if hh % 2 != g:
            col = pltpu.roll(col, HEAD_DIM, 1)
        slabs.append(jnp.where(in_half, col * (SCALE * LOG2E), 0.0))
    return jnp.concatenate(slabs, axis=0).astype(BF16)


def _assemble_heads(heads):
    left = lax.broadcasted_iota(jnp.int32, heads[0].shape, 1) < HEAD_DIM
    cols = []
    for c in range(N_HEADS // 2):
        pair = []
        for hh in (2 * c, 2 * c + 1):
            a = heads[hh]
            pair.append(a if hh % 2 == hh // HG else pltpu.roll(a, HEAD_DIM, 1))
        cols.append(jnp.where(left, pair[0], pair[1]))
    return cols


def _v_ext(v, g):
    keep = (lax.broadcasted_iota(jnp.int32, v.shape, 1) // HEAD_DIM) == g
    return jnp.where(keep, v, 1.0).astype(BF16)


def _normalise(acc):
    return acc / pltpu.roll(acc, HEAD_DIM, 1)


def _proj_kernel(x_ref, w_ref, cos_ref, sin_ref,
                 u_ref, q_ref, qr_ref, kc_ref, vc_ref, ks_ref, vs_ref, kw_ref, vw_ref, g_ref):
    xb = x_ref[...].astype(BF16)

    def seg(off, n):
        return _dot(xb, w_ref[:, off:off + n])

    cos = cos_ref[...]
    sin = sin_ref[...]
    u_ref[...] = seg(OFF_UA, C_CONV) * _sigmoid(seg(OFF_UG, C_CONV))
    q = seg(OFF_Q, QW)
    q_ref[...] = q
    for c in range(QW // LANES):
        qr_ref[:, c * LANES:(c + 1) * LANES] = _rope_tile(q[:, c * LANES:(c + 1) * LANES], cos, sin)
    kv = seg(OFF_KC, ZW - OFF_KC)
    part = lambda off: kv[:, off - OFF_KC:off - OFF_KC + KV_W]
    kc_ref[...] = part(OFF_KC)
    vc_ref[...] = part(OFF_VC)
    ks_ref[...] = _rope_tile(part(OFF_KS), cos, sin)
    vs_ref[...] = part(OFF_VS)
    kw_ref[...] = _rope_tile(part(OFF_KW), cos, sin)
    vw_ref[...] = part(OFF_VW)
    g_ref[...] = _sigmoid(part(OFF_GN))


def _proj(x, w, cos_t, sin_t, n_prompt_tiles, tiles_per_seq):
    rows, d = x.shape
    tm = ROW_TILE

    def tab_map(i):
        return (jnp.where(i < n_prompt_tiles, i % tiles_per_seq, tiles_per_seq), 0)

    row = lambda n: pl.BlockSpec((tm, n), lambda i: (i, 0))
    widths = (C_CONV, QW, QW) + (KV_W,) * 6 + (LANES,)
    return pl.pallas_call(
        _proj_kernel,
        grid=(rows // tm,),
        in_specs=[row(d), pl.BlockSpec((d, ZW), lambda i: (0, 0)),
                  pl.BlockSpec((tm, LANES), tab_map), pl.BlockSpec((tm, LANES), tab_map)],
        out_specs=[row(n) for n in widths],
        out_shape=[jax.ShapeDtypeStruct((rows, n), F32) for n in widths],
        compiler_params=_params("parallel"),
        name="proj_in",
    )(x, w, cos_t, sin_t)


def _conv_prompt_kernel(u_ref, w_ref, b_ref, g_ref, beta_ref, c_ref, ext_ref):
    t_len = u_ref.shape[0]
    hist = ext_ref.shape[0] - t_len
    ext_ref[0:hist, :] = jnp.zeros((hist, C_CONV), F32)
    ext_ref[hist:, :] = u_ref[...]
    first = hist - (CONV_W - 1)

    def body(i, carry):
        t0 = pl.multiple_of(i * CONV_TT, CONV_TT)
        n_win = CONV_TT + hist
        win = ext_ref[pl.ds(t0, n_win), :]
        acc = jnp.zeros((CONV_TT, C_CONV), F32) + b_ref[...]
        for r in range(SUBLANES):
            shifted = win if r == 0 else pltpu.roll(win, n_win - r, 0)
            for k in range(CONV_W):
                if (first + k) % SUBLANES == r:
                    a = first + k - r
                    acc = acc + shifted[a:a + CONV_TT] * w_ref[k:k + 1, :]
        c_ref[pl.ds(t0, CONV_TT), :] = _silu(_layer_norm(acc, g_ref[...], beta_ref[...]))
        return carry

    lax.fori_loop(0, t_len // CONV_TT, body, 0)


def _conv_prompt(u, w, b, g, beta, n_seq, t_len):
    hist = 32
    vec = pl.BlockSpec((1, C_CONV), lambda i: (0, 0))
    return pl.pallas_call(
        _conv_prompt_kernel,
        grid=(n_seq,),
        in_specs=[pl.BlockSpec((t_len, C_CONV), lambda i: (i, 0)),
                  pl.BlockSpec((CONV_W, C_CONV), lambda i: (0, 0)), vec, vec, vec],
        out_specs=pl.BlockSpec((t_len, C_CONV), lambda i: (i, 0)),
        out_shape=jax.ShapeDtypeStruct((n_seq * t_len, C_CONV), F32),
        scratch_shapes=[pltpu.VMEM((hist + t_len, C_CONV), F32)],
        compiler_params=_params("parallel"),
        name="conv_prompt",
    )(u, w, b, g, beta)


def _conv_sample_kernel(st_ref, u_ref, w_ref, b_ref, g_ref, beta_ref, c_ref):
    n_prev = st_ref.shape[0]
    n_new = u_ref.shape[0]
    for t in range(n_new):
        acc = jnp.zeros(c_ref.shape[1:], F32) + b_ref[...]
        for k in range(CONV_W):
            j = t + k
            row = st_ref[j] if j < n_prev else u_ref[j - n_prev]
            acc = acc + row * w_ref[k:k + 1, :]
        c_ref[t] = _silu(_layer_norm(acc, g_ref[...], beta_ref[...]))


def _conv_sample(state_t, u_t, w, b, g, beta):
    return pl.pallas_call(
        _conv_sample_kernel,
        out_shape=jax.ShapeDtypeStruct(u_t.shape, F32),
        compiler_params=pltpu.CompilerParams(vmem_limit_bytes=VMEM_LIMIT),
        name="conv_sample",
    )(state_t, u_t, w, b, g, beta)


def _compress_core(ch, pe_ref, w1_ref, w2_ref):
    n = ch.shape[0]
    hcat = _dot(ch.astype(BF16), w1_ref[...])
    pcat = sum(_dot(part, w1_ref[...]) for part in _split_bf16(pe_ref[...], 2))
    h0 = hcat[:, :LANES] + pcat[0:1, :LANES]
    h1 = hcat[:, LANES:] + pcat[1:2, LANES:]
    h = h0 + pltpu.roll(h1, n - 1, 0)
    return _dot(_gelu_tanh(h).astype(BF16), w2_ref[...])


def _compress_prompt_kernel(ch_ref, pe_ref, w1_ref, w2_ref, o_ref):
    o_ref[0] = _compress_core(ch_ref[...], pe_ref, w1_ref, w2_ref)


def _compress_prompt(rows, pe2, w1b, w2b, n_seq, t_len):
    n_ch = t_len // CMP_STRIDE
    kdim = CMP_STRIDE * KV_W
    ch = rows.reshape(rows.shape[0] // CMP_STRIDE, kdim)
    return pl.pallas_call(
        _compress_prompt_kernel,
        grid=(n_seq,),
        in_specs=[pl.BlockSpec((n_ch, kdim), lambda i: (i, 0)),
                  pl.BlockSpec((SUBLANES, kdim), lambda i: (0, 0)),
                  pl.BlockSpec((kdim, 2 * LANES), lambda i: (0, 0)),
                  pl.BlockSpec((LANES, LANES), lambda i: (0, 0))],
        out_specs=pl.BlockSpec((1, n_ch, LANES), lambda i: (i, 0, 0)),
        out_shape=jax.ShapeDtypeStruct((n_seq, n_ch, LANES), F32),
        compiler_params=_params("parallel"),
        name="compress_prompt",
    )(ch, pe2, w1b, w2b)


def _pages_native(cache):
    depth, n_pool, page, n_kv, hd = cache.shape
    return jnp.transpose(cache, (0, 1, 3, 4, 2)).reshape(depth, n_pool, n_kv * hd, page)


def _page_copy(cache_ref, buf_ref, sem_ref, layer, page_id, slot, p):
    page = cache_ref.shape[3]
    window = pl.ds(pl.multiple_of(p * page, page), page)
    return pltpu.make_async_copy(cache_ref.at[layer, page_id], buf_ref.at[slot, :, window], sem_ref.at[slot])


def _for_pages(n_pages, need, fn):
    def body(p, carry):
        if need is None:
            fn(p)
        else:
            @pl.when(need(p) != 0)
            def _():
                fn(p)
        return carry
    lax.fori_loop(0, n_pages, body, 0)


def _start_pages(pt_ref, cache_ref, buf_ref, sem_ref, layer, seq, slot, n_pages, need_ref=None):
    need = None if need_ref is None else (lambda p: need_ref[seq, p])
    _for_pages(n_pages, need,
               lambda p: _page_copy(cache_ref, buf_ref, sem_ref, layer, pt_ref[seq, p], slot, p).start())


def _wait_pages(pt_ref, cache_ref, buf_ref, sem_ref, layer, seq, slot, n_pages, need_ref=None):
    need = None if need_ref is None else (lambda p: need_ref[seq, p])
    _for_pages(n_pages, need,
               lambda p: _page_copy(cache_ref, buf_ref, sem_ref, layer, pt_ref[seq, p], slot, p).wait())


def _compress_sample_kernel(pt_ref, cache_ref, pe_ref, w1_ref, w2_ref, o_ref, buf_ref, rows_ref, ch_ref, sem_ref,
                            *, layer):
    b = pl.program_id(0)
    nb = pl.num_programs(0)
    n_pages = buf_ref.shape[2] // cache_ref.shape[3]
    n_ch = ch_ref.shape[0]
    slot = b % 2

    @pl.when(b == 0)
    def _():
        _start_pages(pt_ref, cache_ref, buf_ref, sem_ref, layer, 0, 0, n_pages)

    @pl.when(b + 1 < nb)
    def _():
        _start_pages(pt_ref, cache_ref, buf_ref, sem_ref, layer, b + 1, 1 - slot, n_pages)

    _wait_pages(pt_ref, cache_ref, buf_ref, sem_ref, layer, b, slot, n_pages)

    rows_ref[...] = buf_ref[slot].T
    for s in range(CMP_STRIDE):
        ch_ref[:, s * KV_W:(s + 1) * KV_W] = rows_ref[pl.ds(s, n_ch, stride=CMP_STRIDE), :].astype(BF16)
    o_ref[0] = _compress_core(ch_ref[...], pe_ref, w1_ref, w2_ref)


def _compress_sample(page_table, cache_n, pe2, w1b, w2b, layer):
    page = cache_n.shape[3]
    kdim = CMP_STRIDE * KV_W
    n_seq, n_pages = page_table.shape
    n_ch = n_pages * page // CMP_STRIDE
    grid_spec = pltpu.PrefetchScalarGridSpec(
        num_scalar_prefetch=1,
        grid=(n_seq,),
        in_specs=[pl.BlockSpec(memory_space=pl.ANY),
                  pl.BlockSpec((SUBLANES, kdim), lambda i, pt: (0, 0)),
                  pl.BlockSpec((kdim, 2 * LANES), lambda i, pt: (0, 0)),
                  pl.BlockSpec((LANES, LANES), lambda i, pt: (0, 0))],
        out_specs=pl.BlockSpec((1, n_ch, LANES), lambda i, pt: (i, 0, 0)),
        scratch_shapes=[pltpu.VMEM((2, KV_W, n_pages * page), F32), pltpu.VMEM((n_pages * page, KV_W), F32),
                        pltpu.VMEM((n_ch, kdim), BF16), pltpu.SemaphoreType.DMA((2,))],
    )
    return pl.pallas_call(
        functools.partial(_compress_sample_kernel, layer=layer),
        grid_spec=grid_spec,
        out_shape=jax.ShapeDtypeStruct((n_seq, n_ch, LANES), F32),
        compiler_params=_params("arbitrary"),
        name="compress_sample",
    )(page_table, cache_n, pe2, w1b, w2b)


def _select_mask(score_t, qpos_row, n_slc):
    nq = score_t.shape[1]
    if nq > LANES:
        return jnp.concatenate([_select_mask(score_t[:, c:c + LANES], qpos_row[:, c:c + LANES], n_slc)
                                for c in range(0, nq, LANES)], axis=1)
    j = lax.broadcasted_iota(jnp.int32, score_t.shape, 0)
    cur = qpos_row // SLC_BLK
    forced = (j == 0) | (j == cur) | (j == cur - 1)
    sc = jnp.where(forced, FORCE_SCORE, jnp.where(j <= cur, score_t, -1.0))
    sc = jnp.where(j < n_slc, sc, -2.0)
    n_slabs = score_t.shape[0] // SUBLANES
    slabs = [sc[v * SUBLANES:(v + 1) * SUBLANES] for v in range(n_slabs)]
    ranks = [jnp.zeros(s.shape, F32) for s in slabs]
    for other in range(n_slc):
        row = sc[other:other + 1, :]
        for v, slab in enumerate(slabs):
            if (v + 1) * SUBLANES - 1 <= other:
                beats = row > slab
            elif v * SUBLANES > other:
                beats = row >= slab
            else:
                higher = j[v * SUBLANES:(v + 1) * SUBLANES] > other
                beats = (row > slab) | ((row == slab) & higher)
            ranks[v] = ranks[v] + jnp.where(beats, 1.0, 0.0)
    return (jnp.concatenate(ranks, axis=0) < min(N_SEL, n_slc)).astype(F32)


def _cmp_sel_prompt_kernel(q_ref, kc_ref, vc_ref, g_ref, mt_ref, o_ref, sel_ref, *, n_cmp, n_slc):
    tq = q_ref.shape[0]
    n_pad = kc_ref.shape[1]
    q0 = pl.program_id(1) * tq
    qpos_col = q0 + lax.broadcasted_iota(jnp.int32, (tq, 1), 0)
    qpos_row = q0 + lax.broadcasted_iota(jnp.int32, (1, tq), 1)
    n_idx = lax.broadcasted_iota(jnp.int32, (tq, n_pad), 1)
    mask = (n_idx * CMP_STRIDE + (CMP_LEN - 1) <= qpos_col) & (n_idx < n_cmp)
    any_row = (qpos_col >= CMP_LEN - 1).astype(F32)
    gates = g_ref[...]
    q = q_ref[...]
    kb = kc_ref[0].astype(BF16)
    vb = vc_ref[0].astype(BF16)
    all_causal = (q0 + tq - 1) // SLC_BLK < min(N_SEL, n_slc)
    blk = lax.broadcasted_iota(jnp.int32, (tq, sel_ref.shape[2]), 1)
    heads = [None] * N_HEADS
    for g in range(N_KV):
        s = _dot_nt(_stack_heads(q, g), kb)
        probs = []
        for h in range(HG):
            sh = jnp.where(mask, s[h * tq:(h + 1) * tq], MASK_VALUE)
            e = jnp.exp2(sh - jnp.max(sh, axis=-1, keepdims=True))
            probs.append(e / jnp.sum(e, axis=-1, keepdims=True) * any_row)
        o = _dot(jnp.concatenate([p.astype(BF16) for p in probs], axis=0), vb)
        for h in range(HG):
            hh = g * HG + h
            heads[hh] = o[h * tq:(h + 1) * tq] * gates[:, 3 * hh:3 * hh + 1]
        psum = sum(probs[1:], probs[0])

        @pl.when(all_causal)
        def _():
            sel_ref[g] = (blk <= qpos_col // SLC_BLK).astype(F32)

        @pl.when(jnp.logical_not(all_causal))
        def _():
            score_t = sum(_dot_nt(mt_ref[...], part) for part in _split_bf16(psum, 3))
            sel_t = _select_mask(score_t, qpos_row, n_slc)
            unused = jnp.zeros((sel_ref.shape[2] - sel_t.shape[0], tq), F32)
            sel_ref[g] = jnp.concatenate([sel_t, unused], axis=0).T

    for c, col in enumerate(_assemble_heads(heads)):
        o_ref[:, c * LANES:(c + 1) * LANES] = col


def _cmp_sel_prompt(q, kcmp, vcmp, gates, mt, n_seq, t_len, n_cmp, n_slc):
    tq = ATT_TQ
    nq = t_len // tq
    n_pad = kcmp.shape[1]
    rows = n_seq * t_len
    kv = pl.BlockSpec((1, n_pad, LANES), lambda b, i: (b, 0, 0))
    return pl.pallas_call(
        functools.partial(_cmp_sel_prompt_kernel, n_cmp=n_cmp, n_slc=n_slc),
        grid=(n_seq, nq),
        in_specs=[pl.BlockSpec((tq, QW), lambda b, i: (b * nq + i, 0)), kv, kv,
                  pl.BlockSpec((tq, LANES), lambda b, i: (b * nq + i, 0)),
                  pl.BlockSpec(mt.shape, lambda b, i: (0, 0))],
        out_specs=[pl.BlockSpec((tq, QW), lambda b, i: (b * nq + i, 0)),
                   pl.BlockSpec((N_KV, tq, SEL_ROWS_PROMPT), lambda b, i: (0, b * nq + i, 0))],
        out_shape=[jax.ShapeDtypeStruct((rows, QW), F32),
                   jax.ShapeDtypeStruct((N_KV, rows, SEL_ROWS_PROMPT), F32)],
        compiler_params=_params("parallel", "parallel"),
        name="cmp_select_prompt",
    )(q, kcmp, vcmp, gates, mt)


def _slc_prompt_kernel(qr_ref, ks_ref, vs_ref, sel_ref, e_ref, g_ref, o_ref, qs_ref, m_ref, acc_ref):
    tq = qr_ref.shape[0]
    tk = e_ref.shape[2]
    q0 = pl.program_id(1) * tq
    qpos = q0 + lax.broadcasted_iota(jnp.int32, (tq, 1), 0)
    q = qr_ref[...]
    for g in range(N_KV):
        qs_ref[g] = _stack_heads(q, g)
    m_ref[...] = jnp.full(m_ref.shape, MASK_VALUE, F32)
    acc_ref[...] = jnp.zeros(acc_ref.shape, F32)

    def body(c, carry):
        k0 = pl.multiple_of(c * tk, tk)
        causal = (k0 + lax.broadcasted_iota(jnp.int32, (tq, tk), 1)) <= qpos
        kb = ks_ref[pl.ds(k0, tk), :].astype(BF16)
        v = vs_ref[pl.ds(k0, tk), :]
        for g in range(N_KV):
            chosen = _dot(sel_ref[g].astype(BF16), e_ref[c])
            mask = causal & (chosen > 0.5)
            s = _dot_nt(qs_ref[g], kb)
            probs, alphas = [], []
            for h in range(HG):
                hh = g * HG + h
                sh = jnp.where(mask, s[h * tq:(h + 1) * tq], MASK_VALUE)
                m_old = m_ref[hh]
                m_new = jnp.maximum(m_old, jnp.max(sh, axis=-1, keepdims=True))
                m_ref[hh] = m_new
                probs.append(jnp.exp2(sh - jnp.concatenate([m_new] * (tk // LANES), axis=1)).astype(BF16))
                alphas.append(jnp.exp2(m_old - m_new))
            pv = _dot(jnp.concatenate(probs, axis=0), _v_ext(v, g))
            for h in range(HG):
                hh = g * HG + h
                acc_ref[hh] = alphas[h] * acc_ref[hh] + pv[h * tq:(h + 1) * tq]
        return carry

    lax.fori_loop(0, (q0 + tq + tk - 1) // tk, body, 0)
    gates = g_ref[...]
    heads = [_normalise(acc_ref[hh]) * gates[:, 3 * hh + 1:3 * hh + 2] for hh in range(N_HEADS)]
    for c, col in enumerate(_assemble_heads(heads)):
        o_ref[:, c * LANES:(c + 1) * LANES] = col


def _slc_prompt(qr, ks, vs, sel, expand, gates, n_seq, t_len):
    tq = ATT_TQ
    nq = t_len // tq
    rows = n_seq * t_len
    kv = pl.BlockSpec((t_len, KV_W), lambda b, i: (b, 0))
    return pl.pallas_call(
        _slc_prompt_kernel,
        grid=(n_seq, nq),
        in_specs=[pl.BlockSpec((tq, QW), lambda b, i: (b * nq + i, 0)), kv, kv,
                  pl.BlockSpec((N_KV, tq, SEL_ROWS_PROMPT), lambda b, i: (0, b * nq + i, 0)),
                  pl.BlockSpec(expand.shape, lambda b, i: (0, 0, 0)),
                  pl.BlockSpec((tq, LANES), lambda b, i: (b * nq + i, 0))],
        out_specs=pl.BlockSpec((tq, QW), lambda b, i: (b * nq + i, 0)),
        out_shape=jax.ShapeDtypeStruct((rows, QW), F32),
        scratch_shapes=[pltpu.VMEM((N_KV, HG * tq, LANES), BF16),
                        pltpu.VMEM((N_HEADS, tq, LANES), F32), pltpu.VMEM((N_HEADS, tq, LANES), F32)],
        compiler_params=_params("parallel", "parallel"),
        name="slc_prompt",
    )(qr, ks, vs, sel, expand, gates)


def _win_prompt_kernel(qr_ref, kw_ref, vw_ref, g_ref, o_ref, *, nk):
    tq = qr_ref.shape[0]
    q0 = pl.program_id(1) * tq
    start = pl.multiple_of(jnp.maximum(q0 + tq - nk, 0), tq)
    qpos = q0 + lax.broadcasted_iota(jnp.int32, (tq, 1), 0)
    diff = qpos - (start + lax.broadcasted_iota(jnp.int32, (tq, nk), 1))
    mask = (diff >= 0) & (diff < WINDOW)
    gates = g_ref[...]
    q = qr_ref[...]
    kb = kw_ref[pl.ds(start, nk), :].astype(BF16)
    v = vw_ref[pl.ds(start, nk), :]
    heads = [None] * N_HEADS
    for g in range(N_KV):
        s = _dot_nt(_stack_heads(q, g), kb)
        probs = []
        for h in range(HG):
            sh = jnp.where(mask, s[h * tq:(h + 1) * tq], MASK_VALUE)
            probs.append(jnp.exp2(sh - jnp.max(sh, axis=-1, keepdims=True)).astype(BF16))
        pv = _dot(jnp.concatenate(probs, axis=0), _v_ext(v, g))
        for h in range(HG):
            hh = g * HG + h
            heads[hh] = _normalise(pv[h * tq:(h + 1) * tq]) * gates[:, 3 * hh + 2:3 * hh + 3]
    for c, col in enumerate(_assemble_heads(heads)):
        o_ref[:, c * LANES:(c + 1) * LANES] = col


def _win_prompt(qr, kw, vw, gates, n_seq, t_len):
    tq = ATT_TQ
    nq = t_len // tq
    nk = min(WINDOW + tq, t_len)
    kv = pl.BlockSpec((t_len, KV_W), lambda b, i: (b, 0))
    return pl.pallas_call(
        functools.partial(_win_prompt_kernel, nk=nk),
        grid=(n_seq, nq),
        in_specs=[pl.BlockSpec((tq, QW), lambda b, i: (b * nq + i, 0)), kv, kv,
                  pl.BlockSpec((tq, LANES), lambda b, i: (b * nq + i, 0))],
        out_specs=pl.BlockSpec((tq, QW), lambda b, i: (b * nq + i, 0)),
        out_shape=jax.ShapeDtypeStruct((n_seq * t_len, QW), F32),
        compiler_params=_params("parallel", "parallel"),
        name="win_prompt",
    )(qr, kw, vw, gates)


def _cmp_win_sample_kernel(q_ref, qr_ref, kc_ref, vc_ref, wk_ref, wv_ref, nk_ref, nv_ref, g_ref,
                           o_ref, ps_ref, *, past, n_cmp, n_new):
    nq = q_ref.shape[1]
    n_pad = kc_ref.shape[1]
    wbuf = wk_ref.shape[1]
    qpos_col = past + lax.broadcasted_iota(jnp.int32, (nq, 1), 0)
    n_idx = lax.broadcasted_iota(jnp.int32, (nq, n_pad), 1)
    mask_c = (n_idx * CMP_STRIDE + (CMP_LEN - 1) <= qpos_col) & (n_idx < n_cmp)
    any_row = (qpos_col >= CMP_LEN - 1).astype(F32)
    t_col = lax.broadcasted_iota(jnp.int32, (nq, 1), 0)
    diff_buf = t_col + wbuf - lax.broadcasted_iota(jnp.int32, (nq, wbuf), 1)
    mask_buf = (diff_buf >= 0) & (diff_buf < WINDOW)
    r_idx = lax.broadcasted_iota(jnp.int32, (nq, nq), 1)
    mask_new = (t_col - r_idx >= 0) & (t_col - r_idx < WINDOW) & (r_idx < n_new)
    gates = g_ref[0]
    q = q_ref[0]
    qr = qr_ref[0]
    kcb = kc_ref[0].astype(BF16)
    vcb = vc_ref[0].astype(BF16)
    wkb = wk_ref[0].astype(BF16)
    nkb = nk_ref[0].astype(BF16)
    heads = [None] * N_HEADS
    rows = lambda a, h: a[h * nq:(h + 1) * nq]
    stack = lambda parts: jnp.concatenate(parts, axis=0).astype(BF16)
    for g in range(N_KV):
        sc = _dot_nt(_stack_heads(q, g), kcb)
        qs = _stack_heads(qr, g)
        s1 = _dot_nt(qs, wkb)
        s2 = _dot_nt(qs, nkb)
        probs, e1s, e2s = [], [], []
        for h in range(HG):
            sh = jnp.where(mask_c, rows(sc, h), MASK_VALUE)
            e = jnp.exp2(sh - jnp.max(sh, axis=-1, keepdims=True))
            probs.append(e / jnp.sum(e, axis=-1, keepdims=True) * any_row)
            a1 =jnp.where(mask_buf, rows(s1, h), MASK_VALUE)
            a2 = jnp.where(mask_new, rows(s2, h), MASK_VALUE)
            m = jnp.maximum(jnp.max(a1, axis=-1, keepdims=True), jnp.max(a2, axis=-1, keepdims=True))
            e1s.append(jnp.exp2(a1 - m))
            e2s.append(jnp.exp2(a2 - m))
        o_cmp = _dot(stack(probs), vcb)
        o_win = _normalise(_dot(stack(e1s), _v_ext(wv_ref[0], g)) + _dot(stack(e2s), _v_ext(nv_ref[0], g)))
        for h in range(HG):
            hh = g * HG + h
            heads[hh] = (rows(o_cmp, h) * gates[:, 3 * hh:3 * hh + 1]
                         + rows(o_win, h) * gates[:, 3 * hh + 2:3 * hh + 3])
        ps_ref[0, g] = sum(probs[1:], probs[0])
    for c, col in enumerate(_assemble_heads(heads)):
        o_ref[0, :, c * LANES:(c + 1) * LANES] = col


def _cmp_win_sample(q, qr, kcmp, vcmp, wk, wv, nk, nv, gates, past, n_cmp, n_new):
    n_seq, nq, _ = q.shape
    n_pad = kcmp.shape[1]
    per_seq = lambda a: pl.BlockSpec((1,) + a.shape[1:], lambda b: (b, 0, 0))
    return pl.pallas_call(
        functools.partial(_cmp_win_sample_kernel, past=past, n_cmp=n_cmp, n_new=n_new),
        grid=(n_seq,),
        in_specs=[per_seq(a) for a in (q, qr, kcmp, vcmp, wk, wv, nk, nv, gates)],
        out_specs=[pl.BlockSpec((1, nq, QW), lambda b: (b, 0, 0)),
                   pl.BlockSpec((1, N_KV, nq, n_pad), lambda b: (b, 0, 0, 0))],
        out_shape=[jax.ShapeDtypeStruct((n_seq, nq, QW), F32),
                   jax.ShapeDtypeStruct((n_seq, N_KV, nq, n_pad), F32)],
        compiler_params=_params("parallel"),
        name="cmp_win_sample",
    )(q, qr, kcmp, vcmp, wk, wv, nk, nv, gates)


def _select_sample_kernel(ps_ref, mt_ref, e_ref, selx_ref, *, past, n_slc, n_new):
    nq_all = ps_ref.shape[1]
    chunk = 8 * LANES
    tok_row = lax.broadcasted_iota(jnp.int32, (1, nq_all), 1) % DEC_PAD
    live = (lax.broadcasted_iota(jnp.int32, (nq_all, 1), 0) % DEC_PAD < n_new).astype(F32)
    for g in range(N_KV):
        score_t = sum(_dot_nt(mt_ref[...], part) for part in _split_bf16(ps_ref[g], 3))
        sel_t = _select_mask(score_t, past + tok_row, n_slc)
        unused = jnp.zeros((e_ref.shape[0] - sel_t.shape[0], nq_all), F32)
        sel = (jnp.concatenate([sel_t, unused], axis=0).T * live).astype(BF16)
        for c in range(past // chunk):
            selx_ref[g, :, c * chunk:(c + 1) * chunk] = _dot(sel, e_ref[:, c * chunk:(c + 1) * chunk])


def _select_sample(psum, mt, expand, past, n_slc, n_new):
    assert past % (8 * LANES) == 0
    return pl.pallas_call(
        functools.partial(_select_sample_kernel, past=past, n_slc=n_slc, n_new=n_new),
        out_shape=jax.ShapeDtypeStruct((N_KV, psum.shape[1], past), F32),
        compiler_params=pltpu.CompilerParams(vmem_limit_bytes=VMEM_LIMIT),
        name="select_sample",
    )(psum, mt, expand)


def _slc_sample_kernel(pt_ref, need_ref, qr_ref, selx_ref, nk_ref, nv_ref, g_ref, part_ref, kcache_ref, vcache_ref,
                       o_ref, kbuf_ref, vbuf_ref, ksem_ref, vsem_ref, *, layer, n_new):
    b = pl.program_id(0)
    nb = pl.num_programs(0)
    n_pages = kbuf_ref.shape[2] // kcache_ref.shape[3]
    nq = qr_ref.shape[1]
    slot = b % 2

    def start(seq, sl):
        _start_pages(pt_ref, kcache_ref, kbuf_ref, ksem_ref, layer, seq, sl, n_pages, need_ref)
        _start_pages(pt_ref, vcache_ref, vbuf_ref, vsem_ref, layer, seq, sl, n_pages, need_ref)

    @pl.when(b == 0)
    def _():
        kbuf_ref[...] = jnp.zeros(kbuf_ref.shape, F32)
        vbuf_ref[...] = jnp.zeros(vbuf_ref.shape, F32)
        start(0, 0)

    @pl.when(b + 1 < nb)
    def _():
        start(b + 1, 1 - slot)

    _wait_pages(pt_ref, kcache_ref, kbuf_ref, ksem_ref, layer, b, slot, n_pages, need_ref)
    _wait_pages(pt_ref, vcache_ref, vbuf_ref, vsem_ref, layer, b, slot, n_pages, need_ref)

    q = qr_ref[0]
    qs = jnp.concatenate([_stack_heads(q, g) for g in range(N_KV)], axis=0)
    s_past = _dot(qs, kbuf_ref[slot].astype(BF16))
    s_new = _dot_nt(qs, nk_ref[0].astype(BF16))
    t_col = lax.broadcasted_iota(jnp.int32, (nq, 1), 0)
    r_idx = lax.broadcasted_iota(jnp.int32, (nq, nq), 1)
    mask_new = (r_idx <= t_col) & (r_idx < n_new)
    e_past, e_new, dens = [], [], []
    for hh in range(N_HEADS):
        rs = slice(hh * nq, (hh + 1) * nq)
        chosen = selx_ref[hh // HG] > 0.5
        sm = jnp.where(chosen, s_past[rs], MASK_VALUE)
        sn = jnp.where(mask_new, s_new[rs], MASK_VALUE)
        m = jnp.maximum(jnp.max(sm, axis=-1, keepdims=True), jnp.max(sn, axis=-1, keepdims=True))
        e = jnp.exp2(sm - m)
        en = jnp.exp2(sn - m)
        e_past.append(e)
        e_new.append(en)
        dens.append(jnp.sum(e, axis=-1, keepdims=True) + jnp.sum(en, axis=-1, keepdims=True))
    acc = (_dot_nt(jnp.concatenate(e_past, axis=0).astype(BF16), vbuf_ref[slot].astype(BF16))
           + _dot(jnp.concatenate(e_new, axis=0).astype(BF16), nv_ref[0].astype(BF16)))
    o = acc / jnp.concatenate(dens, axis=0)
    gates = g_ref[0]
    heads = [o[hh * nq:(hh + 1) * nq] * gates[:, 3 * hh + 1:3 * hh + 2] for hh in range(N_HEADS)]
    for c, col in enumerate(_assemble_heads(heads)):
        o_ref[0, :, c * LANES:(c + 1) * LANES] = part_ref[0, :, c * LANES:(c + 1) * LANES] + col


def _slc_sample(page_table, need, qr, selx, nk, nv, gates, part, kcache_n, vcache_n, layer, n_new):
    n_seq, nq, _ = qr.shape
    n_pages = page_table.shape[1]
    page = kcache_n.shape[3]
    per_seq = lambda a: pl.BlockSpec((1,) + a.shape[1:], lambda b, pt, nd: (b, 0, 0))
    grid_spec = pltpu.PrefetchScalarGridSpec(
        num_scalar_prefetch=2,
        grid=(n_seq,),
        in_specs=[per_seq(qr), pl.BlockSpec((N_KV, nq, n_pages * page), lambda b, pt, nd: (0, b, 0))]
                 + [per_seq(a) for a in (nk, nv, gates, part)]
                 + [pl.BlockSpec(memory_space=pl.ANY), pl.BlockSpec(memory_space=pl.ANY)],
        out_specs=pl.BlockSpec((1, nq, QW), lambda b, pt, nd: (b, 0, 0)),
        scratch_shapes=[pltpu.VMEM((2, KV_W, n_pages * page), F32), pltpu.VMEM((2, KV_W, n_pages * page), F32),
                        pltpu.SemaphoreType.DMA((2,)), pltpu.SemaphoreType.DMA((2,))],
    )
    return pl.pallas_call(
        functools.partial(_slc_sample_kernel, layer=layer, n_new=n_new),
        grid_spec=grid_spec,
        out_shape=jax.ShapeDtypeStruct((n_seq, nq, QW), F32),
        compiler_params=_params("arbitrary"),
        name="slc_sample",
    )(page_table, need, qr, selx, nk, nv, gates, part, kcache_n, vcache_n)


def _merge_kernel(x_ref, cp_ref, cs_ref, op1_ref, op2_ref, op3_ref, os_ref,
                  wga_ref, wgb_ref, wco_ref, wao_ref, wo_ref, g_ref, b_ref, y_ref, *, n_prompt_tiles, alpha):
    is_prompt = pl.program_id(0) < n_prompt_tiles
    x = x_ref[...]
    xb = x.astype(BF16)
    c = jnp.where(is_prompt, cp_ref[...], cs_ref[...])
    o = jnp.where(is_prompt, op1_ref[...] + op2_ref[...] + op3_ref[...], os_ref[...])
    mixed = _sigmoid(_dot(xb, wga_ref[...])) * _dot(c.astype(BF16), wco_ref[...])
    mixed = mixed + _sigmoid(_dot(xb, wgb_ref[...])) * _dot(o.astype(BF16), wao_ref[...])
    y = _dot(mixed.astype(BF16), wo_ref[...])
    y_ref[...] = _layer_norm(alpha * x + y, g_ref[...], b_ref[...])


def _merge(x, c_p, c_s, o_cmp, o_slc, o_win, o_s, wga, wgb, wco, wao, wo, g, b, alpha):
    rows, d = x.shape
    tm = MERGE_TILE
    n_pt = c_p.shape[0] // tm
    prm = lambda n: pl.BlockSpec((tm, n), lambda i: (jnp.minimum(i, n_pt - 1), 0))
    smp = lambda n: pl.BlockSpec((tm, n), lambda i: (jnp.maximum(i - n_pt, 0), 0))
    full = lambda a: pl.BlockSpec(a.shape, lambda i: (0, 0))
    return pl.pallas_call(
        functools.partial(_merge_kernel, n_prompt_tiles=n_pt, alpha=alpha),
        grid=(rows // tm,),
        in_specs=[pl.BlockSpec((tm, d), lambda i: (i, 0)), prm(C_CONV), smp(C_CONV),
                  prm(QW), prm(QW), prm(QW), smp(QW),
                  full(wga), full(wgb), full(wco), full(wao), full(wo), full(g), full(b)],
        out_specs=pl.BlockSpec((tm, d), lambda i: (i, 0)),
        out_shape=jax.ShapeDtypeStruct((rows, d), F32),
        compiler_params=_params("parallel"),
        name="merge",
    )(x, c_p, c_s, o_cmp, o_slc, o_win, o_s, wga, wgb, wco, wao, wo, g, b)


def _ffn_kernel(x_ref, wg_ref, wu_ref, wd_ref, g_ref, b_ref, y_ref, xb_ref, acc_ref, *, alpha):
    j = pl.program_id(1)

    @pl.when(j == 0)
    def _():
        xb_ref[...] = x_ref[...].astype(BF16)
        acc_ref[...] = jnp.zeros(acc_ref.shape, F32)

    xb = xb_ref[...]
    h = _silu(_dot(xb, wg_ref[...])) * _dot(xb, wu_ref[...])
    acc_ref[...] += _dot(h.astype(BF16), wd_ref[...])

    @pl.when(j == pl.num_programs(1) - 1)
    def _():
        y_ref[...] = _layer_norm(alpha * x_ref[...] + acc_ref[...], g_ref[...], b_ref[...])


def _ffn(x, wg, wu, wd, g, b, alpha):
    rows, d = x.shape
    d_ff = wg.shape[1]
    tm = ROW_TILE
    tf = d_ff // 2 if d_ff % (2 * LANES) == 0 else LANES
    vec = pl.BlockSpec((1, d), lambda i, j: (0, 0))
    return pl.pallas_call(
        functools.partial(_ffn_kernel, alpha=alpha),
        grid=(rows // tm, d_ff // tf),
        in_specs=[pl.BlockSpec((tm, d), lambda i, j: (i, 0)),
                  pl.BlockSpec((d, tf), lambda i, j: (0, j)), pl.BlockSpec((d, tf), lambda i, j: (0, j)),
                  pl.BlockSpec((tf, d), lambda i, j: (j, 0)), vec, vec],
        out_specs=pl.BlockSpec((tm, d), lambda i, j: (i, 0)),
        out_shape=jax.ShapeDtypeStruct((rows, d), F32),
        scratch_shapes=[pltpu.VMEM((tm, d), BF16), pltpu.VMEM((tm, d), F32)],
        compiler_params=_params("parallel", "arbitrary"),
        name="ffn",
    )(x, wg, wu, wd, g, b)


def _rope_tables(pos):
    half = HEAD_DIM // 2
    inv = ROPE_THETA ** (-jnp.arange(half, dtype=F32) / half)
    ang = pos.astype(F32)[:, None] * inv[None, :]
    cos, sin = jnp.cos(ang), jnp.sin(ang)
    reps = LANES // HEAD_DIM
    return (jnp.tile(jnp.concatenate([cos, cos], -1), (1, reps)),
            jnp.tile(jnp.concatenate([-sin, sin], -1), (1, reps)))


def _overlap_t(n_cmp, n_slc, rows, cols):
    i = np.arange(n_cmp)[:, None]
    j = np.arange(n_slc)[None, :]
    ov = np.minimum(i * CMP_STRIDE + CMP_LEN, (j + 1) * SLC_BLK) - np.maximum(i * CMP_STRIDE, j * SLC_BLK)
    m = np.zeros((rows, cols), np.float32)
    m[:n_slc, :n_cmp] = (np.clip(ov, 0, None) / CMP_LEN).T
    return jnp.asarray(m, dtype=BF16)


def _expand_table(rows, n_keys):
    e = (np.arange(n_keys)[None, :] // SLC_BLK) == np.arange(rows)[:, None]
    return jnp.asarray(e.astype(np.float32), dtype=BF16)


def _compress_weights(pe, w1, w2):
    r = CMP_LEN // CMP_STRIDE
    eye = jnp.eye(N_KV, dtype=F32)
    w1_r = w1.reshape(r, CMP_STRIDE, HEAD_DIM, CMP_HID)
    w1b = jnp.einsum('msde,gh->sgdmhe', w1_r, eye).reshape(CMP_STRIDE * KV_W, r * N_KV * CMP_HID).astype(BF16)
    w2b = jnp.einsum('ed,gh->gehd', w2, eye).reshape(N_KV * CMP_HID, KV_W).astype(BF16)
    pe_r = pe.reshape(r, CMP_STRIDE, 1, HEAD_DIM)
    pe2 = jnp.broadcast_to(pe_r, (r, CMP_STRIDE, N_KV, HEAD_DIM)).reshape(r, CMP_STRIDE * KV_W)
    pe2 = jnp.pad(pe2, ((0, SUBLANES - r), (0, 0)))
    return pe2, w1b, w2b


def _pad_rows(a, n):
    return jnp.pad(a, ((0, 0), (0, n - a.shape[1]), (0, 0)))


def kernel(x_prompt, x_sample, cache_cmp_k, cache_cmp_v, cache_slc_k, cache_slc_v, state_win_k, state_win_v, state_conv, page_table, w_in, cmp_pe_k, cmp_w1_k, cmp_w2_k, cmp_pe_v, cmp_w1_v, cmp_w2_v, conv_w, conv_b, conv_ln_g, conv_ln_b, w_conv_out, w_attn_out, w_o, ln1_g, ln1_b, w_gate, w_up, w_down, ln2_g, ln2_b):
    n_seq, t_len, d = x_prompt.shape
    db, dt, _ = x_sample.shape
    depth = w_in.shape[0]
    page = cache_cmp_k.shape[2]
    past = page_table.shape[1] * page
    wbuf = state_win_k.shape[2]
    alpha = (2.0 * depth) ** 0.25
    np_rows = n_seq * t_len
    ns_rows = db * dt
    assert t_len % ROW_TILE == 0 and ns_rows <= ROW_TILE and dt <= DEC_PAD
    assert QW == 4 * LANES and past % CMP_STRIDE == 0 and past >= CMP_LEN
    n_prompt_tiles = np_rows // ROW_TILE
    tiles_per_seq = t_len // ROW_TILE

    x = jnp.concatenate([x_prompt.reshape(np_rows, d), x_sample.reshape(ns_rows, d),
                         jnp.zeros((ROW_TILE - ns_rows, d), F32)], axis=0)
    pos = jnp.concatenate([jnp.arange(t_len), past + (jnp.arange(ROW_TILE) % dt)])
    cos_t, sin_t = _rope_tables(pos)

    n_cmp_p = (t_len - CMP_LEN) // CMP_STRIDE + 1
    n_slc_p = -(-t_len // SLC_BLK)
    assert n_slc_p <= SEL_ROWS_PROMPT and t_len // CMP_STRIDE <= LANES
    mt_p = _overlap_t(n_cmp_p, n_slc_p, -(-n_slc_p // SUBLANES) * SUBLANES, t_len // CMP_STRIDE)
    exp_p = _expand_table(SEL_ROWS_PROMPT, t_len).reshape(SEL_ROWS_PROMPT, t_len // SLC_TK, SLC_TK).transpose(1, 0, 2)
    n_all = past + dt
    n_cmp_s = (n_all - CMP_LEN) // CMP_STRIDE + 1
    n_slc_s = -(-n_all // SLC_BLK)
    n_ch_s = past // CMP_STRIDE
    assert n_cmp_s + 1 <= n_ch_s and n_slc_s <= SEL_ROWS_SAMPLE and (n_slc_s - 1) * SLC_BLK == past
    mt_s = _overlap_t(n_cmp_s, n_slc_s, -(-n_slc_s // SUBLANES) * SUBLANES, n_ch_s)
    exp_s = _expand_table(SEL_ROWS_SAMPLE, past)
    n_pages = page_table.shape[1]
    assert page % SLC_BLK == 0 and KV_W == LANES
    cmp_k_n, cmp_v_n, slc_k_n, slc_v_n = (_pages_native(c) for c in (cache_cmp_k, cache_cmp_v, cache_slc_k, cache_slc_v))

    bf = lambda a: a.astype(BF16)
    outs_p = [[] for _ in range(7)]
    outs_s = [[] for _ in range(7)]
    for l in range(depth):
        w_z = bf(w_in[l, :, :ZW])
        w_ga = bf(w_in[l, :, OFF_GA:OFF_GA + d])
        w_gb = bf(w_in[l, :, OFF_GA + d:OFF_GA + 2 * d])
        u, q, qr, kc, vc, ks, vs, kw, vw, gates = _proj(x, w_z, cos_t, sin_t, n_prompt_tiles, tiles_per_seq)

        smp = lambda a: a[np_rows:np_rows + ns_rows].reshape(db, dt, a.shape[-1])
        smp8 = lambda a: _pad_rows(smp(a), DEC_PAD)
        row2 = lambda v: v.reshape(1, -1)

        cw, cb, cg, cbeta = conv_w[l], row2(conv_b[l]), row2(conv_ln_g[l]), row2(conv_ln_b[l])
        c_p = _conv_prompt(u, cw, cb, cg, cbeta, n_seq, t_len)
        u_s = smp(u)
        c_s = _conv_sample(state_conv[l].transpose(1, 0, 2), u_s.transpose(1, 0, 2), cw, cb, cg, cbeta)
        c_s = jnp.pad(c_s.transpose(1, 0, 2).reshape(ns_rows, C_CONV), ((0, ROW_TILE - ns_rows), (0, 0)))

        pe2k, w1k, w2k = _compress_weights(cmp_pe_k[l], cmp_w1_k[l], cmp_w2_k[l])
        pe2v, w1v, w2v = _compress_weights(cmp_pe_v[l], cmp_w1_v[l], cmp_w2_v[l])
        kcmp_p = _compress_prompt(kc, pe2k, w1k, w2k, n_seq, t_len)
        vcmp_p = _compress_prompt(vc, pe2v, w1v, w2v, n_seq, t_len)
        kcmp_s = _compress_sample(page_table, cmp_k_n, pe2k, w1k, w2k, l)
        vcmp_s = _compress_sample(page_table, cmp_v_n, pe2v, w1v, w2v, l)

        o_cmp, sel_p = _cmp_sel_prompt(q, kcmp_p, vcmp_p, gates, mt_p, n_seq, t_len, n_cmp_p, n_slc_p)
        o_slc = _slc_prompt(qr, ks, vs, sel_p, exp_p, gates, n_seq, t_len)
        o_win = _win_prompt(qr, kw, vw, gates, n_seq, t_len)

        ks_s, vs_s, kw_s, vw_s = smp(ks), smp(vs), smp(kw), smp(vw)
        q8, qr8, g8 = smp8(q), smp8(qr), smp8(gates)
        wk = state_win_k[l].reshape(db, wbuf, KV_W)
        wv = state_win_v[l].reshape(db, wbuf, KV_W)
        part, psum_s = _cmp_win_sample(q8, qr8, kcmp_s, vcmp_s, wk, wv, _pad_rows(kw_s, DEC_PAD),
                                       _pad_rows(vw_s, DEC_PAD), g8, past, n_cmp_s, dt)
        psum_s = psum_s.transpose(1, 0, 2, 3).reshape(N_KV, db * DEC_PAD, n_ch_s)
        selx = _select_sample(psum_s, mt_s, exp_s, past, n_slc_s, dt)
        need = (selx.reshape(N_KV, db, DEC_PAD, n_pages, page).max(axis=(0, 2, 4)) > 0.5).astype(jnp.int32)
        o_s8 = _slc_sample(page_table, need, qr8, selx, _pad_rows(ks_s, DEC_PAD), _pad_rows(vs_s, DEC_PAD), g8, part,
                           slc_k_n, slc_v_n, l, dt)
        o_s = jnp.pad(o_s8[:, :dt].reshape(ns_rows, QW), ((0, ROW_TILE - ns_rows), (0, 0)))

        x = _merge(x, c_p, c_s, o_cmp, o_slc, o_win, o_s, w_ga, w_gb, bf(w_conv_out[l]), bf(w_attn_out[l]),
                   bf(w_o[l]), row2(ln1_g[l]), row2(ln1_b[l]), alpha)
        x = _ffn(x, bf(w_gate[l]), bf(w_up[l]), bf(w_down[l]), row2(ln2_g[l]), row2(ln2_b[l]), alpha)

        prm = lambda a: a[:np_rows].reshape(n_seq, t_len, N_KV, HEAD_DIM)
        kv4 = lambda a: a.reshape(db, -1, N_KV, HEAD_DIM)
        n_keep = min(WINDOW, t_len)
        new_p = (u[:np_rows].reshape(n_seq, t_len, C_CONV)[:, t_len - (CONV_W - 1):], prm(kc), prm(vc), prm(ks), prm(vs),
                 prm(kw)[:, t_len - n_keep:], prm(vw)[:, t_len - n_keep:])
        new_s = (jnp.concatenate([state_conv[l], u_s], axis=1)[:, -(CONV_W - 1):],
                 kv4(smp(kc)), kv4(smp(vc)), kv4(ks_s), kv4(vs_s),
                 jnp.concatenate([state_win_k[l], kv4(kw_s)], axis=1)[:, dt:],
                 jnp.concatenate([state_win_v[l], kv4(vw_s)], axis=1)[:, dt:])
        for i in range(7):
            outs_p[i].append(new_p[i])
            outs_s[i].append(new_s[i])

    stk = lambda rows: jnp.stack(rows, axis=0)
    order = (1, 2, 3, 4, 5, 6, 0)
    return ((x[:np_rows].reshape(n_seq, t_len, d), x[np_rows:np_rows + ns_rows].reshape(db, dt, d))
            + tuple(stk(outs_p[i]) for i in order) + tuple(stk(outs_s[i]) for i in order))
```

```python
import functools
import math

import numpy as np
import jax
import jax.numpy as jnp
from jax import lax
from jax.experimental import pallas as pl
from jax.experimental.pallas import tpu as pltpu

F32 = jnp.float32
BF16 = jnp.bfloat16

C_CONV = 512
CONV_W = 31
N_HEADS = 8
N_KV = 2
HG = N_HEADS // N_KV
HEAD_DIM = 64
CMP_LEN = 32
CMP_STRIDE = 16
CMP_HID = 64
SLC_BLK = 64
N_SEL = 16
WINDOW = 512
ROPE_THETA = 10000.0
LN_EPS = 1e-5
MASK_VALUE = -1e30
FORCE_SCORE = 1e4
KV_W = N_KV * HEAD_DIM
QW = N_HEADS * HEAD_DIM
SCALE = HEAD_DIM ** -0.5
LOG2E = math.log2(math.e)

LANES = 128
SUBLANES = 8
VMEM_LIMIT = 56 * 1024 * 1024

OFF_UA = 0
OFF_UG = C_CONV
OFF_Q = 2 * C_CONV
OFF_KC = OFF_Q + QW
OFF_VC = OFF_KC + KV_W
OFF_KS = OFF_VC + KV_W
OFF_VS = OFF_KS + KV_W
OFF_KW = OFF_VS + KV_W
OFF_VW = OFF_KW + KV_W
OFF_GN = OFF_VW + KV_W
OFF_GA = OFF_GN + 3 * N_HEADS
ZW = OFF_GN + LANES

ROW_TILE = 512
MERGE_TILE = 256
DEC_PAD = 8
ATT_TQ = 256
SLC_TK = 512
CONV_TT = 64
SEL_ROWS_PROMPT = 128
SEL_ROWS_SAMPLE = 256


def _params(*sem):
    return pltpu.CompilerParams(dimension_semantics=sem, vmem_limit_bytes=VMEM_LIMIT)


def _sigmoid(x):
    return 1.0 / (1.0 + jnp.exp(-x))


def _silu(x):
    return x * _sigmoid(x)


def _gelu_tanh(x):
    return 0.5 * x * (1.0 + jnp.tanh(math.sqrt(2.0 / math.pi) * (x + 0.044715 * (x * x * x))))


def _layer_norm(y, g, b):
    mu = jnp.mean(y, axis=-1, keepdims=True)
    d = y - mu
    var = jnp.mean(d * d, axis=-1, keepdims=True)
    return d * lax.rsqrt(var + LN_EPS) * g + b


def _dot(a, b):
    return jnp.dot(a, b, preferred_element_type=F32)


def _dot_nt(a, b):
    return lax.dot_general(a, b, (((1,), (1,)), ((), ())), preferred_element_type=F32)


def _split_bf16(a, terms):
    parts, rest = [], a
    for _ in range(terms):
        part = rest.astype(BF16)
        parts.append(part)
        rest = rest - part.astype(F32)
    return parts


def _rope_tile(x, cos, sin_signed):
    lane = lax.broadcasted_iota(jnp.int32, x.shape, 1)
    first = (lane % HEAD_DIM) < (HEAD_DIM // 2)
    rot = jnp.where(first, pltpu.roll(x, LANES - HEAD_DIM // 2, 1), pltpu.roll(x, HEAD_DIM // 2, 1))
    return x * cos + rot * sin_signed


def _stack_heads(q, g):
    tq = q.shape[0]
    in_half = (lax.broadcasted_iota(jnp.int32, (tq, LANES), 1) // HEAD_DIM) == g
    slabs = []
    for h in range(HG):
        hh = g * HG + h
        col = q[:, (hh // 2) * LANES:(hh // 2 + 1) * LANES]
        if hh % 2 != g:
            col = pltpu.roll(col, HEAD_DIM, 1)
        slabs.append(jnp.where(in_half, col * (SCALE * LOG2E), 0.0))
    return jnp.concatenate(slabs, axis=0).astype(BF16)


def _assemble_heads(heads):
    left = lax.broadcasted_iota(jnp.int32, heads[0].shape, 1) < HEAD_DIM
    cols = []
    for c in range(N_HEADS // 2):
        pair = []
        for hh in (2 * c, 2 * c + 1):
            a = heads[hh]
            pair.append(a if hh % 2 == hh // HG else pltpu.roll(a, HEAD_DIM, 1))
        cols.append(jnp.where(left, pair[0], pair[1]))
    return cols


def _v_ext(v, g):
    keep = (lax.broadcasted_iota(jnp.int32, v.shape, 1) // HEAD_DIM) == g
    return jnp.where(keep, v, 1.0).astype(BF16)


def _normalise(acc):
    return acc / pltpu.roll(acc, HEAD_DIM, 1)


def _proj_kernel(x_ref, w_ref, cos_ref, sin_ref,
                 u_ref, q_ref, qr_ref, kc_ref, vc_ref, ks_ref, vs_ref, kw_ref, vw_ref, g_ref):
    xb = x_ref[...].astype(BF16)

    def seg(off, n):
        return _dot(xb, w_ref[:, off:off + n])

    cos = cos_ref[...]
    sin = sin_ref[...]
    u_ref[...] = seg(OFF_UA, C_CONV) * _sigmoid(seg(OFF_UG, C_CONV))
    q = seg(OFF_Q, QW)
    q_ref[...] = q
    for c in range(QW // LANES):
        qr_ref[:, c * LANES:(c + 1) * LANES] = _rope_tile(q[:, c * LANES:(c + 1) * LANES], cos, sin)
    kv = seg(OFF_KC, ZW - OFF_KC)
    part = lambda off: kv[:, off - OFF_KC:off - OFF_KC + KV_W]
    kc_ref[...] = part(OFF_KC)
    vc_ref[...] = part(OFF_VC)
    ks_ref[...] = _rope_tile(part(OFF_KS), cos, sin)
    vs_ref[...] = part(OFF_VS)
    kw_ref[...] = _rope_tile(part(OFF_KW), cos, sin)
    vw_ref[...] = part(OFF_VW)
    g_ref[...] = _sigmoid(part(OFF_GN))


def _proj(x, w, cos_t, sin_t, n_prompt_tiles, tiles_per_seq):
    rows, d = x.shape
    tm = ROW_TILE

    def tab_map(i):
        return (jnp.where(i < n_prompt_tiles, i % tiles_per_seq, tiles_per_seq), 0)

    row = lambda n: pl.BlockSpec((tm, n), lambda i: (i, 0))
    widths = (C_CONV, QW, QW) + (KV_W,) * 6 + (LANES,)
    return pl.pallas_call(
        _proj_kernel,
        grid=(rows // tm,),
        in_specs=[row(d), pl.BlockSpec((d, ZW), lambda i: (0, 0)),
                  pl.BlockSpec((tm, LANES), tab_map), pl.BlockSpec((tm, LANES), tab_map)],
        out_specs=[row(n) for n in widths],
        out_shape=[jax.ShapeDtypeStruct((rows, n), F32) for n in widths],
        compiler_params=_params("parallel"),
        name="proj_in",
    )(x, w, cos_t, sin_t)


def _conv_prompt_kernel(u_ref, w_ref, b_ref, g_ref, beta_ref, c_ref, ext_ref):
    t_len = u_ref.shape[0]
    hist = ext_ref.shape[0] - t_len
    ext_ref[0:hist, :] = jnp.zeros((hist, C_CONV), F32)
    ext_ref[hist:, :] = u_ref[...]
    first = hist - (CONV_W - 1)

    def body(i, carry):
        t0 = pl.multiple_of(i * CONV_TT, CONV_TT)
        n_win = CONV_TT + hist
        win = ext_ref[pl.ds(t0, n_win), :]
        acc = jnp.zeros((CONV_TT, C_CONV), F32) + b_ref[...]
        for r in range(SUBLANES):
            shifted = win if r == 0 else pltpu.roll(win, n_win - r, 0)
            for k in range(CONV_W):
                if (first + k) % SUBLANES == r:
                    a = first + k - r
                    acc = acc + shifted[a:a + CONV_TT] * w_ref[k:k + 1, :]
        c_ref[pl.ds(t0, CONV_TT), :] = _silu(_layer_norm(acc, g_ref[...], beta_ref[...]))
        return carry

    lax.fori_loop(0, t_len // CONV_TT, body, 0)


def _conv_prompt(u, w, b, g, beta, n_seq, t_len):
    hist = 32
    vec = pl.BlockSpec((1, C_CONV), lambda i: (0, 0))
    return pl.pallas_call(
        _conv_prompt_kernel,
        grid=(n_seq,),
        in_specs=[pl.BlockSpec((t_len, C_CONV), lambda i: (i, 0)),
                  pl.BlockSpec((CONV_W, C_CONV), lambda i: (0, 0)), vec, vec, vec],
        out_specs=pl.BlockSpec((t_len, C_CONV), lambda i: (i, 0)),
        out_shape=jax.ShapeDtypeStruct((n_seq * t_len, C_CONV), F32),
        scratch_shapes=[pltpu.VMEM((hist + t_len, C_CONV), F32)],
        compiler_params=_params("parallel"),
        name="conv_prompt",
    )(u, w, b, g, beta)


def _conv_sample_kernel(st_ref, u_ref, w_ref, b_ref, g_ref, beta_ref, c_ref):
    n_prev = st_ref.shape[0]
    n_new = u_ref.shape[0]
    for t in range(n_new):
        acc = jnp.zeros(c_ref.shape[1:], F32) + b_ref[...]
        for k in range(CONV_W):
            j = t + k
            row = st_ref[j] if j < n_prev else u_ref[j - n_prev]
            acc = acc + row * w_ref[k:k + 1, :]
        c_ref[t] = _silu(_layer_norm(acc, g_ref[...], beta_ref[...]))


def _conv_sample(state_t, u_t, w, b, g, beta):
    return pl.pallas_call(
        _conv_sample_kernel,
        out_shape=jax.ShapeDtypeStruct(u_t.shape, F32),
        compiler_params=pltpu.CompilerParams(vmem_limit_bytes=VMEM_LIMIT),
        name="conv_sample",
    )(state_t, u_t, w, b, g, beta)


def _compress_core(ch, pe_ref, w1_ref, w2_ref):
    n = ch.shape[0]
    hcat = _dot(ch.astype(BF16), w1_ref[...])
    pcat = sum(_dot(part, w1_ref[...]) for part in _split_bf16(pe_ref[...], 2))
    h0 = hcat[:, :LANES] + pcat[0:1, :LANES]
    h1 = hcat[:, LANES:] + pcat[1:2, LANES:]
    h = h0 + pltpu.roll(h1, n - 1, 0)
    return _dot(_gelu_tanh(h).astype(BF16), w2_ref[...])


def _compress_prompt_kernel(ch_ref, pe_ref, w1_ref, w2_ref, o_ref):
    o_ref[0] = _compress_core(ch_ref[...], pe_ref, w1_ref, w2_ref)


def _compress_prompt(rows, pe2, w1b, w2b, n_seq, t_len):
    n_ch = t_len // CMP_STRIDE
    kdim = CMP_STRIDE * KV_W
    ch = rows.reshape(rows.shape[0] // CMP_STRIDE, kdim)
    return pl.pallas_call(
        _compress_prompt_kernel,
        grid=(n_seq,),
        in_specs=[pl.BlockSpec((n_ch, kdim), lambda i: (i, 0)),
                  pl.BlockSpec((SUBLANES, kdim), lambda i: (0, 0)),
                  pl.BlockSpec((kdim, 2 * LANES), lambda i: (0, 0)),
                  pl.BlockSpec((LANES, LANES), lambda i: (0, 0))],
        out_specs=pl.BlockSpec((1, n_ch, LANES), lambda i: (i, 0, 0)),
        out_shape=jax.ShapeDtypeStruct((n_seq, n_ch, LANES), F32),
        compiler_params=_params("parallel"),
        name="compress_prompt",
    )(ch, pe2, w1b, w2b)


def _pages_native(cache):
    depth, n_pool, page, n_kv, hd = cache.shape
    return jnp.transpose(cache, (0, 1, 3, 4, 2)).reshape(depth, n_pool, n_kv * hd, page)


def _page_copy(cache_ref, buf_ref, sem_ref, layer, page_id, slot, p):
    page = cache_ref.shape[3]
    window = pl.ds(pl.multiple_of(p * page, page), page)
    return pltpu.make_async_copy(cache_ref.at[layer, page_id], buf_ref.at[slot, :, window], sem_ref.at[slot])


def _for_pages(n_pages, need, fn):
    def body(p, carry):
        if need is None:
            fn(p)
        else:
            @pl.when(need(p) != 0)
            def _():
                fn(p)
        return carry
    lax.fori_loop(0, n_pages, body, 0)


def _start_pages(pt_ref, cache_ref, buf_ref, sem_ref, layer, seq, slot, n_pages, need_ref=None):
    need = None if need_ref is None else (lambda p: need_ref[seq, p])
    _for_pages(n_pages, need,
               lambda p: _page_copy(cache_ref, buf_ref, sem_ref, layer, pt_ref[seq, p], slot, p).start())


def _wait_pages(pt_ref, cache_ref, buf_ref, sem_ref, layer, seq, slot, n_pages, need_ref=None):
    need = None if need_ref is None else (lambda p: need_ref[seq, p])
    _for_pages(n_pages, need,
               lambda p: _page_copy(cache_ref, buf_ref, sem_ref, layer, pt_ref[seq, p], slot, p).wait())


def _compress_sample_kernel(pt_ref, cache_ref, pe_ref, w1_ref, w2_ref, o_ref, buf_ref, rows_ref, ch_ref, sem_ref,
                            *, layer):
    b = pl.program_id(0)
    nb = pl.num_programs(0)
    n_pages = buf_ref.shape[2] // cache_ref.shape[3]
    n_ch = ch_ref.shape[0]
    slot = b % 2

    @pl.when(b == 0)
    def _():
        _start_pages(pt_ref, cache_ref, buf_ref, sem_ref, layer, 0, 0, n_pages)

    @pl.when(b + 1 < nb)
    def _():
        _start_pages(pt_ref, cache_ref, buf_ref, sem_ref, layer, b + 1, 1 - slot, n_pages)

    _wait_pages(pt_ref, cache_ref, buf_ref, sem_ref, layer, b, slot, n_pages)

    rows_ref[...] = buf_ref[slot].T
    for s in range(CMP_STRIDE):
        ch_ref[:, s * KV_W:(s + 1) * KV_W] = rows_ref[pl.ds(s, n_ch, stride=CMP_STRIDE), :].astype(BF16)
    o_ref[0] = _compress_core(ch_ref[...], pe_ref, w1_ref, w2_ref)


def _compress_sample(page_table, cache_n, pe2, w1b, w2b, layer):
    page = cache_n.shape[3]
    kdim = CMP_STRIDE * KV_W
    n_seq, n_pages = page_table.shape
    n_ch = n_pages * page // CMP_STRIDE
    grid_spec = pltpu.PrefetchScalarGridSpec(
        num_scalar_prefetch=1,
        grid=(n_seq,),
        in_specs=[pl.BlockSpec(memory_space=pl.ANY),
                  pl.BlockSpec((SUBLANES, kdim), lambda i, pt: (0, 0)),
                  pl.BlockSpec((kdim, 2 * LANES), lambda i, pt: (0, 0)),
                  pl.BlockSpec((LANES, LANES), lambda i, pt: (0, 0))],
        out_specs=pl.BlockSpec((1, n_ch, LANES), lambda i, pt: (i, 0, 0)),
        scratch_shapes=[pltpu.VMEM((2, KV_W, n_pages * page), F32), pltpu.VMEM((n_pages * page, KV_W), F32),
                        pltpu.VMEM((n_ch, kdim), BF16), pltpu.SemaphoreType.DMA((2,))],
    )
    return pl.pallas_call(
        functools.partial(_compress_sample_kernel, layer=layer),
        grid_spec=grid_spec,
        out_shape=jax.ShapeDtypeStruct((n_seq, n_ch, LANES), F32),
        compiler_params=_params("arbitrary"),
        name="compress_sample",
    )(page_table, cache_n, pe2, w1b, w2b)


def _select_mask(score_t, qpos_row, n_slc):
    nq = score_t.shape[1]
    if nq > LANES:
        return jnp.concatenate([_select_mask(score_t[:, c:c + LANES], qpos_row[:, c:c + LANES], n_slc)
                                for c in range(0, nq, LANES)], axis=1)
    j = lax.broadcasted_iota(jnp.int32, score_t.shape, 0)
    cur = qpos_row // SLC_BLK
    forced = (j == 0) | (j == cur) | (j == cur - 1)
    sc = jnp.where(forced, FORCE_SCORE, jnp.where(j <= cur, score_t, -1.0))
    sc = jnp.where(j < n_slc, sc, -2.0)
    n_slabs = score_t.shape[0] // SUBLANES
    slabs = [sc[v * SUBLANES:(v + 1) * SUBLANES] for v in range(n_slabs)]
    ranks = [jnp.zeros(s.shape, F32) for s in slabs]
    for other in range(n_slc):
        row = sc[other:other + 1, :]
        for v, slab in enumerate(slabs):
            if (v + 1) * SUBLANES - 1 <= other:
                beats = row > slab
            elif v * SUBLANES > other:
                beats = row >= slab
            else:
                higher = j[v * SUBLANES:(v + 1) * SUBLANES] > other
                beats = (row > slab) | ((row == slab) & higher)
            ranks[v] = ranks[v] + jnp.where(beats, 1.0, 0.0)
    return (jnp.concatenate(ranks, axis=0) < min(N_SEL, n_slc)).astype(F32)


def _cmp_sel_prompt_kernel(q_ref, kc_ref, vc_ref, g_ref, mt_ref, o_ref, sel_ref, *, n_cmp, n_slc):
    tq = q_ref.shape[0]
    n_pad = kc_ref.shape[1]
    q0 = pl.program_id(1) * tq
    qpos_col = q0 + lax.broadcasted_iota(jnp.int32, (tq, 1), 0)
    qpos_row = q0 + lax.broadcasted_iota(jnp.int32, (1, tq), 1)
    n_idx = lax.broadcasted_iota(jnp.int32, (tq, n_pad), 1)
    mask = (n_idx * CMP_STRIDE + (CMP_LEN - 1) <= qpos_col) & (n_idx < n_cmp)
    any_row = (qpos_col >= CMP_LEN - 1).astype(F32)
    gates = g_ref[...]
    q = q_ref[...]
    kb = kc_ref[0].astype(BF16)
    vb = vc_ref[0].astype(BF16)
    all_causal = (q0 + tq - 1) // SLC_BLK < min(N_SEL, n_slc)
    blk = lax.broadcasted_iota(jnp.int32, (tq, sel_ref.shape[2]), 1)
    heads = [None] * N_HEADS
    for g in range(N_KV):
        s = _dot_nt(_stack_heads(q, g), kb)
        probs = []
        for h in range(HG):
            sh = jnp.where(mask, s[h * tq:(h + 1) * tq], MASK_VALUE)
            e = jnp.exp2(sh - jnp.max(sh, axis=-1, keepdims=True))
            probs.append(e / jnp.sum(e, axis=-1, keepdims=True) * any_row)
        o = _dot(jnp.concatenate([p.astype(BF16) for p in probs], axis=0), vb)
        for h in range(HG):
            hh = g * HG + h
            heads[hh] = o[h * tq:(h + 1) * tq] * gates[:, 3 * hh:3 * hh + 1]
        psum = sum(probs[1:], probs[0])

        @pl.when(all_causal)
        def _():
            sel_ref[g] = (blk <= qpos_col // SLC_BLK).astype(F32)

        @pl.when(jnp.logical_not(all_causal))
        def _():
            score_t = sum(_dot_nt(mt_ref[...], part) for part in _split_bf16(psum, 3))
            sel_t = _select_mask(score_t, qpos_row, n_slc)
            unused = jnp.zeros((sel_ref.shape[2] - sel_t.shape[0], tq), F32)
            sel_ref[g] = jnp.concatenate([sel_t, unused], axis=0).T

    for c, col in enumerate(_assemble_heads(heads)):
        o_ref[:, c * LANES:(c + 1) * LANES] = col


def _cmp_sel_prompt(q, kcmp, vcmp, gates, mt, n_seq, t_len, n_cmp, n_slc):
    tq = ATT_TQ
    nq = t_len // tq
    n_pad = kcmp.shape[1]
    rows = n_seq * t_len
    kv = pl.BlockSpec((1, n_pad, LANES), lambda b, i: (b, 0, 0))
    return pl.pallas_call(
        functools.partial(_cmp_sel_prompt_kernel, n_cmp=n_cmp, n_slc=n_slc),
        grid=(n_seq, nq),
        in_specs=[pl.BlockSpec((tq, QW), lambda b, i: (b * nq + i, 0)), kv, kv,
                  pl.BlockSpec((tq, LANES), lambda b, i: (b * nq + i, 0)),
                  pl.BlockSpec(mt.shape, lambda b, i: (0, 0))],
        out_specs=[pl.BlockSpec((tq, QW), lambda b, i: (b * nq + i, 0)),
                   pl.BlockSpec((N_KV, tq, SEL_ROWS_PROMPT), lambda b, i: (0, b * nq + i, 0))],
        out_shape=[jax.ShapeDtypeStruct((rows, QW), F32),
                   jax.ShapeDtypeStruct((N_KV, rows, SEL_ROWS_PROMPT), F32)],
        compiler_params=_params("parallel", "parallel"),
        name="cmp_select_prompt",
    )(q, kcmp, vcmp, gates, mt)


def _slc_prompt_kernel(qr_ref, ks_ref, vs_ref, sel_ref, e_ref, g_ref, o_ref, qs_ref, m_ref, acc_ref):
    tq = qr_ref.shape[0]
    tk = e_ref.shape[2]
    q0 = pl.program_id(1) * tq
    qpos = q0 + lax.broadcasted_iota(jnp.int32, (tq, 1), 0)
    q = qr_ref[...]
    for g in range(N_KV):
        qs_ref[g] = _stack_heads(q, g)
    m_ref[...] = jnp.full(m_ref.shape, MASK_VALUE, F32)
    acc_ref[...] = jnp.zeros(acc_ref.shape, F32)

    def body(c, carry):
        k0 = pl.multiple_of(c * tk, tk)
        causal = (k0 + lax.broadcasted_iota(jnp.int32, (tq, tk), 1)) <= qpos
        kb = ks_ref[pl.ds(k0, tk), :].astype(BF16)
        v = vs_ref[pl.ds(k0, tk), :]
        for g in range(N_KV):
            chosen = _dot(sel_ref[g].astype(BF16), e_ref[c])
            mask = causal & (chosen > 0.5)
            s = _dot_nt(qs_ref[g], kb)
            probs, alphas = [], []
            for h in range(HG):
                hh = g * HG + h
                sh = jnp.where(mask, s[h * tq:(h + 1) * tq], MASK_VALUE)
                m_old = m_ref[hh]
                m_new = jnp.maximum(m_old, jnp.max(sh, axis=-1, keepdims=True))
                m_ref[hh] = m_new
                probs.append(jnp.exp2(sh - jnp.concatenate([m_new] * (tk // LANES), axis=1)).astype(BF16))
                alphas.append(jnp.exp2(m_old - m_new))
            pv = _dot(jnp.concatenate(probs, axis=0), _v_ext(v, g))
            for h in range(HG):
                hh = g * HG + h
                acc_ref[hh] = alphas[h] * acc_ref[hh] + pv[h * tq:(h + 1) * tq]
        return carry

    lax.fori_loop(0, (q0 + tq + tk - 1) // tk, body, 0)
    gates = g_ref[...]
    heads = [_normalise(acc_ref[hh]) * gates[:, 3 * hh + 1:3 * hh + 2] for hh in range(N_HEADS)]
    for c, col in enumerate(_assemble_heads(heads)):
        o_ref[:, c * LANES:(c + 1) * LANES] = col


def _slc_prompt(qr, ks, vs, sel, expand, gates, n_seq, t_len):
    tq = ATT_TQ
    nq = t_len // tq
    rows = n_seq * t_len
    kv = pl.BlockSpec((t_len, KV_W), lambda b, i: (b, 0))
    return pl.pallas_call(
        _slc_prompt_kernel,
        grid=(n_seq, nq),
        in_specs=[pl.BlockSpec((tq, QW), lambda b, i: (b * nq + i, 0)), kv, kv,
                  pl.BlockSpec((N_KV, tq, SEL_ROWS_PROMPT), lambda b, i: (0, b * nq + i, 0)),
                  pl.BlockSpec(expand.shape, lambda b, i: (0, 0, 0)),
                  pl.BlockSpec((tq, LANES), lambda b, i: (b * nq + i, 0))],
        out_specs=pl.BlockSpec((tq, QW), lambda b, i: (b * nq + i, 0)),
        out_shape=jax.ShapeDtypeStruct((rows, QW), F32),
        scratch_shapes=[pltpu.VMEM((N_KV, HG * tq, LANES), BF16),
                        pltpu.VMEM((N_HEADS, tq, LANES), F32), pltpu.VMEM((N_HEADS, tq, LANES), F32)],
        compiler_params=_params("parallel", "parallel"),
        name="slc_prompt",
    )(qr, ks, vs, sel, expand, gates)


def _win_prompt_kernel(qr_ref, kw_ref, vw_ref, g_ref, o_ref, *, nk):
    tq = qr_ref.shape[0]
    q0 = pl.program_id(1) * tq
    start = pl.multiple_of(jnp.maximum(q0 + tq - nk, 0), tq)
    qpos = q0 + lax.broadcasted_iota(jnp.int32, (tq, 1), 0)
    diff = qpos - (start + lax.broadcasted_iota(jnp.int32, (tq, nk), 1))
    mask = (diff >= 0) & (diff < WINDOW)
    gates = g_ref[...]
    q = qr_ref[...]
    kb = kw_ref[pl.ds(start, nk), :].astype(BF16)
    v = vw_ref[pl.ds(start, nk), :]
    heads = [None] * N_HEADS
    for g in range(N_KV):
        s = _dot_nt(_stack_heads(q, g), kb)
        probs = []
        for h in range(HG):
            sh = jnp.where(mask, s[h * tq:(h + 1) * tq], MASK_VALUE)
            probs.append(jnp.exp2(sh - jnp.max(sh, axis=-1, keepdims=True)).astype(BF16))
        pv = _dot(jnp.concatenate(probs, axis=0), _v_ext(v, g))
        for h in range(HG):
            hh = g * HG + h
            heads[hh] = _normalise(pv[h * tq:(h + 1) * tq]) * gates[:, 3 * hh + 2:3 * hh + 3]
    for c, col in enumerate(_assemble_heads(heads)):
        o_ref[:, c * LANES:(c + 1) * LANES] = col


def _win_prompt(qr, kw, vw, gates, n_seq, t_len):
    tq = ATT_TQ
    nq = t_len // tq
    nk = min(WINDOW + tq, t_len)
    kv = pl.BlockSpec((t_len, KV_W), lambda b, i: (b, 0))
    return pl.pallas_call(
        functools.partial(_win_prompt_kernel, nk=nk),
        grid=(n_seq, nq),
        in_specs=[pl.BlockSpec((tq, QW), lambda b, i: (b * nq + i, 0)), kv, kv,
                  pl.BlockSpec((tq, LANES), lambda b, i: (b * nq + i, 0))],
        out_specs=pl.BlockSpec((tq, QW), lambda b, i: (b * nq + i, 0)),
        out_shape=jax.ShapeDtypeStruct((n_seq * t_len, QW), F32),
        compiler_params=_params("parallel", "parallel"),
        name="win_prompt",
    )(qr, kw, vw, gates)


def _cmp_win_sample_kernel(q_ref, qr_ref, kc_ref, vc_ref, wk_ref, wv_ref, nk_ref, nv_ref, g_ref,
                           o_ref, ps_ref, *, past, n_cmp, n_new):
    nq = q_ref.shape[1]
    n_pad = kc_ref.shape[1]
    wbuf = wk_ref.shape[1]
    qpos_col = past + lax.broadcasted_iota(jnp.int32, (nq, 1), 0)
    n_idx = lax.broadcasted_iota(jnp.int32, (nq, n_pad), 1)
    mask_c = (n_idx * CMP_STRIDE + (CMP_LEN - 1) <= qpos_col) & (n_idx < n_cmp)
    any_row = (qpos_col >= CMP_LEN - 1).astype(F32)
    t_col = lax.broadcasted_iota(jnp.int32, (nq, 1), 0)
    diff_buf = t_col + wbuf - lax.broadcasted_iota(jnp.int32, (nq, wbuf), 1)
    mask_buf = (diff_buf >= 0) & (diff_buf < WINDOW)
    r_idx = lax.broadcasted_iota(jnp.int32, (nq, nq), 1)
    mask_new = (t_col - r_idx >= 0) & (t_col - r_idx < WINDOW) & (r_idx < n_new)
    gates = g_ref[0]
    q = q_ref[0]
    qr = qr_ref[0]
    kcb = kc_ref[0].astype(BF16)
    vcb = vc_ref[0].astype(BF16)
    wkb = wk_ref[0].astype(BF16)
    nkb = nk_ref[0].astype(BF16)
    heads = [None] * N_HEADS
    rows = lambda a, h: a[h * nq:(h + 1) * nq]
    stack = lambda parts: jnp.concatenate(parts, axis=0).astype(BF16)
    for g in range(N_KV):
        sc = _dot_nt(_stack_heads(q, g), kcb)
        qs = _stack_heads(qr, g)
        s1 = _dot_nt(qs, wkb)
        s2 = _dot_nt(qs, nkb)
        probs, e1s, e2s = [], [], []
        for h in range(HG):
            sh = jnp.where(mask_c, rows(sc, h), MASK_VALUE)
            e = jnp.exp2(sh - jnp.max(sh, axis=-1, keepdims=True))
            probs.append(e / jnp.sum(e, axis=-1, keepdims=True) * any_row)
            a1 =jnp.where(mask_buf, rows(s1, h), MASK_VALUE)
            a2 = jnp.where(mask_new, rows(s2, h), MASK_VALUE)
            m = jnp.maximum(jnp.max(a1, axis=-1, keepdims=True), jnp.max(a2, axis=-1, keepdims=True))
            e1s.append(jnp.exp2(a1 - m))
            e2s.append(jnp.exp2(a2 - m))
        o_cmp = _dot(stack(probs), vcb)
        o_win = _normalise(_dot(stack(e1s), _v_ext(wv_ref[0], g)) + _dot(stack(e2s), _v_ext(nv_ref[0], g)))
        for h in range(HG):
            hh = g * HG + h
            heads[hh] = (rows(o_cmp, h) * gates[:, 3 * hh:3 * hh + 1]
                         + rows(o_win, h) * gates[:, 3 * hh + 2:3 * hh + 3])
        ps_ref[0, g] = sum(probs[1:], probs[0])
    for c, col in enumerate(_assemble_heads(heads)):
        o_ref[0, :, c * LANES:(c + 1) * LANES] = col


def _cmp_win_sample(q, qr, kcmp, vcmp, wk, wv, nk, nv, gates, past, n_cmp, n_new):
    n_seq, nq, _ = q.shape
    n_pad = kcmp.shape[1]
    per_seq = lambda a: pl.BlockSpec((1,) + a.shape[1:], lambda b: (b, 0, 0))
    return pl.pallas_call(
        functools.partial(_cmp_win_sample_kernel, past=past, n_cmp=n_cmp, n_new=n_new),
        grid=(n_seq,),
        in_specs=[per_seq(a) for a in (q, qr, kcmp, vcmp, wk, wv, nk, nv, gates)],
        out_specs=[pl.BlockSpec((1, nq, QW), lambda b: (b, 0, 0)),
                   pl.BlockSpec((1, N_KV, nq, n_pad), lambda b: (b, 0, 0, 0))],
        out_shape=[jax.ShapeDtypeStruct((n_seq, nq, QW), F32),
                   jax.ShapeDtypeStruct((n_seq, N_KV, nq, n_pad), F32)],
        compiler_params=_params("parallel"),
        name="cmp_win_sample",
    )(q, qr, kcmp, vcmp, wk, wv, nk, nv, gates)


def _select_sample_kernel(ps_ref, mt_ref, e_ref, selx_ref, *, past, n_slc, n_new):
    nq_all = ps_ref.shape[1]
    chunk = 8 * LANES
    tok_row = lax.broadcasted_iota(jnp.int32, (1, nq_all), 1) % DEC_PAD
    live = (lax.broadcasted_iota(jnp.int32, (nq_all, 1), 0) % DEC_PAD < n_new).astype(F32)
    for g in range(N_KV):
        score_t = sum(_dot_nt(mt_ref[...], part) for part in _split_bf16(ps_ref[g], 3))
        sel_t = _select_mask(score_t, past + tok_row, n_slc)
        unused = jnp.zeros((e_ref.shape[0] - sel_t.shape[0], nq_all), F32)
        sel = (jnp.concatenate([sel_t, unused], axis=0).T * live).astype(BF16)
        for c in range(past // chunk):
            selx_ref[g, :, c * chunk:(c + 1) * chunk] = _dot(sel, e_ref[:, c * chunk:(c + 1) * chunk])


def _select_sample(psum, mt, expand, past, n_slc, n_new):
    assert past % (8 * LANES) == 0
    return pl.pallas_call(
        functools.partial(_select_sample_kernel, past=past, n_slc=n_slc, n_new=n_new),
        out_shape=jax.ShapeDtypeStruct((N_KV, psum.shape[1], past), F32),
        compiler_params=pltpu.CompilerParams(vmem_limit_bytes=VMEM_LIMIT),
        name="select_sample",
    )(psum, mt, expand)


def _slc_sample_kernel(pt_ref, need_ref, qr_ref, selx_ref, nk_ref, nv_ref, g_ref, part_ref, kcache_ref, vcache_ref,
                       o_ref, kbuf_ref, vbuf_ref, ksem_ref, vsem_ref, *, layer, n_new):
    b = pl.program_id(0)
    nb = pl.num_programs(0)
    n_pages = kbuf_ref.shape[2] // kcache_ref.shape[3]
    nq = qr_ref.shape[1]
    slot = b % 2

    def start(seq, sl):
        _start_pages(pt_ref, kcache_ref, kbuf_ref, ksem_ref, layer, seq, sl, n_pages, need_ref)
        _start_pages(pt_ref, vcache_ref, vbuf_ref, vsem_ref, layer, seq, sl, n_pages, need_ref)

    @pl.when(b == 0)
    def _():
        kbuf_ref[...] = jnp.zeros(kbuf_ref.shape, F32)
        vbuf_ref[...] = jnp.zeros(vbuf_ref.shape, F32)
        start(0, 0)

    @pl.when(b + 1 < nb)
    def _():
        start(b + 1, 1 - slot)

    _wait_pages(pt_ref, kcache_ref, kbuf_ref, ksem_ref, layer, b, slot, n_pages, need_ref)
    _wait_pages(pt_ref, vcache_ref, vbuf_ref, vsem_ref, layer, b, slot, n_pages, need_ref)

    q = qr_ref[0]
    qs = jnp.concatenate([_stack_heads(q, g) for g in range(N_KV)], axis=0)
    s_past = _dot(qs, kbuf_ref[slot].astype(BF16))
    s_new = _dot_nt(qs, nk_ref[0].astype(BF16))
    t_col = lax.broadcasted_iota(jnp.int32, (nq, 1), 0)
    r_idx = lax.broadcasted_iota(jnp.int32, (nq, nq), 1)
    mask_new = (r_idx <= t_col) & (r_idx < n_new)
    e_past, e_new, dens = [], [], []
    for hh in range(N_HEADS):
        rs = slice(hh * nq, (hh + 1) * nq)
        chosen = selx_ref[hh // HG] > 0.5
        sm = jnp.where(chosen, s_past[rs], MASK_VALUE)
        sn = jnp.where(mask_new, s_new[rs], MASK_VALUE)
        m = jnp.maximum(jnp.max(sm, axis=-1, keepdims=True), jnp.max(sn, axis=-1, keepdims=True))
        e = jnp.exp2(sm - m)
        en = jnp.exp2(sn - m)
        e_past.append(e)
        e_new.append(en)
        dens.append(jnp.sum(e, axis=-1, keepdims=True) + jnp.sum(en, axis=-1, keepdims=True))
    acc = (_dot_nt(jnp.concatenate(e_past, axis=0).astype(BF16), vbuf_ref[slot].astype(BF16))
           + _dot(jnp.concatenate(e_new, axis=0).astype(BF16), nv_ref[0].astype(BF16)))
    o = acc / jnp.concatenate(dens, axis=0)
    gates = g_ref[0]
    heads = [o[hh * nq:(hh + 1) * nq] * gates[:, 3 * hh + 1:3 * hh + 2] for hh in range(N_HEADS)]
    for c, col in enumerate(_assemble_heads(heads)):
        o_ref[0, :, c * LANES:(c + 1) * LANES] = part_ref[0, :, c * LANES:(c + 1) * LANES] + col


def _slc_sample(page_table, need, qr, selx, nk, nv, gates, part, kcache_n, vcache_n, layer, n_new):
    n_seq, nq, _ = qr.shape
    n_pages = page_table.shape[1]
    page = kcache_n.shape[3]
    per_seq = lambda a: pl.BlockSpec((1,) + a.shape[1:], lambda b, pt, nd: (b, 0, 0))
    grid_spec = pltpu.PrefetchScalarGridSpec(
        num_scalar_prefetch=2,
        grid=(n_seq,),
        in_specs=[per_seq(qr), pl.BlockSpec((N_KV, nq, n_pages * page), lambda b, pt, nd: (0, b, 0))]
                 + [per_seq(a) for a in (nk, nv, gates, part)]
                 + [pl.BlockSpec(memory_space=pl.ANY), pl.BlockSpec(memory_space=pl.ANY)],
        out_specs=pl.BlockSpec((1, nq, QW), lambda b, pt, nd: (b, 0, 0)),
        scratch_shapes=[pltpu.VMEM((2, KV_W, n_pages * page), F32), pltpu.VMEM((2, KV_W, n_pages * page), F32),
                        pltpu.SemaphoreType.DMA((2,)), pltpu.SemaphoreType.DMA((2,))],
    )
    return pl.pallas_call(
        functools.partial(_slc_sample_kernel, layer=layer, n_new=n_new),
        grid_spec=grid_spec,
        out_shape=jax.ShapeDtypeStruct((n_seq, nq, QW), F32),
        compiler_params=_params("arbitrary"),
        name="slc_sample",
    )(page_table, need, qr, selx, nk, nv, gates, part, kcache_n, vcache_n)


def _merge_kernel(x_ref, cp_ref, cs_ref, op1_ref, op2_ref, op3_ref, os_ref,
                  wga_ref, wgb_ref, wco_ref, wao_ref, wo_ref, g_ref, b_ref, y_ref, *, n_prompt_tiles, alpha):
    is_prompt = pl.program_id(0) < n_prompt_tiles
    x = x_ref[...]
    xb = x.astype(BF16)
    c = jnp.where(is_prompt, cp_ref[...], cs_ref[...])
    o = jnp.where(is_prompt, op1_ref[...] + op2_ref[...] + op3_ref[...], os_ref[...])
    mixed = _sigmoid(_dot(xb, wga_ref[...])) * _dot(c.astype(BF16), wco_ref[...])
    mixed = mixed + _sigmoid(_dot(xb, wgb_ref[...])) * _dot(o.astype(BF16), wao_ref[...])
    y = _dot(mixed.astype(BF16), wo_ref[...])
    y_ref[...] = _layer_norm(alpha * x + y, g_ref[...], b_ref[...])


def _merge(x, c_p, c_s, o_cmp, o_slc, o_win, o_s, wga, wgb, wco, wao, wo, g, b, alpha):
    rows, d = x.shape
    tm = MERGE_TILE
    n_pt = c_p.shape[0] // tm
    prm = lambda n: pl.BlockSpec((tm, n), lambda i: (jnp.minimum(i, n_pt - 1), 0))
    smp = lambda n: pl.BlockSpec((tm, n), lambda i: (jnp.maximum(i - n_pt, 0), 0))
    full = lambda a: pl.BlockSpec(a.shape, lambda i: (0, 0))
    return pl.pallas_call(
        functools.partial(_merge_kernel, n_prompt_tiles=n_pt, alpha=alpha),
        grid=(rows // tm,),
        in_specs=[pl.BlockSpec((tm, d), lambda i: (i, 0)), prm(C_CONV), smp(C_CONV),
                  prm(QW), prm(QW), prm(QW), smp(QW),
                  full(wga), full(wgb), full(wco), full(wao), full(wo), full(g), full(b)],
        out_specs=pl.BlockSpec((tm, d), lambda i: (i, 0)),
        out_shape=jax.ShapeDtypeStruct((rows, d), F32),
        compiler_params=_params("parallel"),
        name="merge",
    )(x, c_p, c_s, o_cmp, o_slc, o_win, o_s, wga, wgb, wco, wao, wo, g, b)


def _ffn_kernel(x_ref, wg_ref, wu_ref, wd_ref, g_ref, b_ref, y_ref, xb_ref, acc_ref, *, alpha):
    j = pl.program_id(1)

    @pl.when(j == 0)
    def _():
        xb_ref[...] = x_ref[...].astype(BF16)
        acc_ref[...] = jnp.zeros(acc_ref.shape, F32)

    xb = xb_ref[...]
    h = _silu(_dot(xb, wg_ref[...])) * _dot(xb, wu_ref[...])
    acc_ref[...] += _dot(h.astype(BF16), wd_ref[...])

    @pl.when(j == pl.num_programs(1) - 1)
    def _():
        y_ref[...] = _layer_norm(alpha * x_ref[...] + acc_ref[...], g_ref[...], b_ref[...])


def _ffn(x, wg, wu, wd, g, b, alpha):
    rows, d = x.shape
    d_ff = wg.shape[1]
    tm = ROW_TILE
    tf = d_ff // 2 if d_ff % (2 * LANES) == 0 else LANES
    vec = pl.BlockSpec((1, d), lambda i, j: (0, 0))
    return pl.pallas_call(
        functools.partial(_ffn_kernel, alpha=alpha),
        grid=(rows // tm, d_ff // tf),
        in_specs=[pl.BlockSpec((tm, d), lambda i, j: (i, 0)),
                  pl.BlockSpec((d, tf), lambda i, j: (0, j)), pl.BlockSpec((d, tf), lambda i, j: (0, j)),
                  pl.BlockSpec((tf, d), lambda i, j: (j, 0)), vec, vec],
        out_specs=pl.BlockSpec((tm, d), lambda i, j: (i, 0)),
        out_shape=jax.ShapeDtypeStruct((rows, d), F32),
        scratch_shapes=[pltpu.VMEM((tm, d), BF16), pltpu.VMEM((tm, d), F32)],
        compiler_params=_params("parallel", "arbitrary"),
        name="ffn",
    )(x, wg, wu, wd, g, b)


def _rope_tables(pos):
    half = HEAD_DIM // 2
    inv = ROPE_THETA ** (-jnp.arange(half, dtype=F32) / half)
    ang = pos.astype(F32)[:, None] * inv[None, :]
    cos, sin = jnp.cos(ang), jnp.sin(ang)
    reps = LANES // HEAD_DIM
    return (jnp.tile(jnp.concatenate([cos, cos], -1), (1, reps)),
            jnp.tile(jnp.concatenate([-sin, sin], -1), (1, reps)))


def _overlap_t(n_cmp, n_slc, rows, cols):
    i = np.arange(n_cmp)[:, None]
    j = np.arange(n_slc)[None, :]
    ov = np.minimum(i * CMP_STRIDE + CMP_LEN, (j + 1) * SLC_BLK) - np.maximum(i * CMP_STRIDE, j * SLC_BLK)
    m = np.zeros((rows, cols), np.float32)
    m[:n_slc, :n_cmp] = (np.clip(ov, 0, None) / CMP_LEN).T
    return jnp.asarray(m, dtype=BF16)


def _expand_table(rows, n_keys):
    e = (np.arange(n_keys)[None, :] // SLC_BLK) == np.arange(rows)[:, None]
    return jnp.asarray(e.astype(np.float32), dtype=BF16)


def _compress_weights(pe, w1, w2):
    r = CMP_LEN // CMP_STRIDE
    eye = jnp.eye(N_KV, dtype=F32)
    w1_r = w1.reshape(r, CMP_STRIDE, HEAD_DIM, CMP_HID)
    w1b = jnp.einsum('msde,gh->sgdmhe', w1_r, eye).reshape(CMP_STRIDE * KV_W, r * N_KV * CMP_HID).astype(BF16)
    w2b = jnp.einsum('ed,gh->gehd', w2, eye).reshape(N_KV * CMP_HID, KV_W).astype(BF16)
    pe_r = pe.reshape(r, CMP_STRIDE, 1, HEAD_DIM)
    pe2 = jnp.broadcast_to(pe_r, (r, CMP_STRIDE, N_KV, HEAD_DIM)).reshape(r, CMP_STRIDE * KV_W)
    pe2 = jnp.pad(pe2, ((0, SUBLANES - r), (0, 0)))
    return pe2, w1b, w2b


def _pad_rows(a, n):
    return jnp.pad(a, ((0, 0), (0, n - a.shape[1]), (0, 0)))


def kernel(x_prompt, x_sample, cache_cmp_k, cache_cmp_v, cache_slc_k, cache_slc_v, state_win_k, state_win_v, state_conv, page_table, w_in, cmp_pe_k, cmp_w1_k, cmp_w2_k, cmp_pe_v, cmp_w1_v, cmp_w2_v, conv_w, conv_b, conv_ln_g, conv_ln_b, w_conv_out, w_attn_out, w_o, ln1_g, ln1_b, w_gate, w_up, w_down, ln2_g, ln2_b):
    n_seq, t_len, d = x_prompt.shape
    db, dt, _ = x_sample.shape
    depth = w_in.shape[0]
    page = cache_cmp_k.shape[2]
    past = page_table.shape[1] * page
    wbuf = state_win_k.shape[2]
    alpha = (2.0 * depth) ** 0.25
    np_rows = n_seq * t_len
    ns_rows = db * dt
    assert t_len % ROW_TILE == 0 and ns_rows <= ROW_TILE and dt <= DEC_PAD
    assert QW == 4 * LANES and past % CMP_STRIDE == 0 and past >= CMP_LEN
    n_prompt_tiles = np_rows // ROW_TILE
    tiles_per_seq = t_len // ROW_TILE

    x = jnp.concatenate([x_prompt.reshape(np_rows, d), x_sample.reshape(ns_rows, d),
                         jnp.zeros((ROW_TILE - ns_rows, d), F32)], axis=0)
    pos = jnp.concatenate([jnp.arange(t_len), past + (jnp.arange(ROW_TILE) % dt)])
    cos_t, sin_t = _rope_tables(pos)

    n_cmp_p = (t_len - CMP_LEN) // CMP_STRIDE + 1
    n_slc_p = -(-t_len // SLC_BLK)
    assert n_slc_p <= SEL_ROWS_PROMPT and t_len // CMP_STRIDE <= LANES
    mt_p = _overlap_t(n_cmp_p, n_slc_p, -(-n_slc_p // SUBLANES) * SUBLANES, t_len // CMP_STRIDE)
    exp_p = _expand_table(SEL_ROWS_PROMPT, t_len).reshape(SEL_ROWS_PROMPT, t_len // SLC_TK, SLC_TK).transpose(1, 0, 2)
    n_all = past + dt
    n_cmp_s = (n_all - CMP_LEN) // CMP_STRIDE + 1
    n_slc_s = -(-n_all // SLC_BLK)
    n_ch_s = past // CMP_STRIDE
    assert n_cmp_s + 1 <= n_ch_s and n_slc_s <= SEL_ROWS_SAMPLE and (n_slc_s - 1) * SLC_BLK == past
    mt_s = _overlap_t(n_cmp_s, n_slc_s, -(-n_slc_s // SUBLANES) * SUBLANES, n_ch_s)
    exp_s = _expand_table(SEL_ROWS_SAMPLE, past)
    n_pages = page_table.shape[1]
    assert page % SLC_BLK == 0 and KV_W == LANES
    cmp_k_n, cmp_v_n, slc_k_n, slc_v_n = (_pages_native(c) for c in (cache_cmp_k, cache_cmp_v, cache_slc_k, cache_slc_v))

    bf = lambda a: a.astype(BF16)
    outs_p = [[] for _ in range(7)]
    outs_s = [[] for _ in range(7)]
    for l in range(depth):
        w_z = bf(w_in[l, :, :ZW])
        w_ga = bf(w_in[l, :, OFF_GA:OFF_GA + d])
        w_gb = bf(w_in[l, :, OFF_GA + d:OFF_GA + 2 * d])
        u, q, qr, kc, vc, ks, vs, kw, vw, gates = _proj(x, w_z, cos_t, sin_t, n_prompt_tiles, tiles_per_seq)

        smp = lambda a: a[np_rows:np_rows + ns_rows].reshape(db, dt, a.shape[-1])
        smp8 = lambda a: _pad_rows(smp(a), DEC_PAD)
        row2 = lambda v: v.reshape(1, -1)

        cw, cb, cg, cbeta = conv_w[l], row2(conv_b[l]), row2(conv_ln_g[l]), row2(conv_ln_b[l])
        c_p = _conv_prompt(u, cw, cb, cg, cbeta, n_seq, t_len)
        u_s = smp(u)
        c_s = _conv_sample(state_conv[l].transpose(1, 0, 2), u_s.transpose(1, 0, 2), cw, cb, cg, cbeta)
        c_s = jnp.pad(c_s.transpose(1, 0, 2).reshape(ns_rows, C_CONV), ((0, ROW_TILE - ns_rows), (0, 0)))

        pe2k, w1k, w2k = _compress_weights(cmp_pe_k[l], cmp_w1_k[l], cmp_w2_k[l])
        pe2v, w1v, w2v = _compress_weights(cmp_pe_v[l], cmp_w1_v[l], cmp_w2_v[l])
        kcmp_p = _compress_prompt(kc, pe2k, w1k, w2k, n_seq, t_len)
        vcmp_p = _compress_prompt(vc, pe2v, w1v, w2v, n_seq, t_len)
        kcmp_s = _compress_sample(page_table, cmp_k_n, pe2k, w1k, w2k, l)
        vcmp_s = _compress_sample(page_table, cmp_v_n, pe2v, w1v, w2v, l)

        o_cmp, sel_p = _cmp_sel_prompt(q, kcmp_p, vcmp_p, gates, mt_p, n_seq, t_len, n_cmp_p, n_slc_p)
        o_slc = _slc_prompt(qr, ks, vs, sel_p, exp_p, gates, n_seq, t_len)
        o_win = _win_prompt(qr, kw, vw, gates, n_seq, t_len)

        ks_s, vs_s, kw_s, vw_s = smp(ks), smp(vs), smp(kw), smp(vw)
        q8, qr8, g8 = smp8(q), smp8(qr), smp8(gates)
        wk = state_win_k[l].reshape(db, wbuf, KV_W)
        wv = state_win_v[l].reshape(db, wbuf, KV_W)
        part, psum_s = _cmp_win_sample(q8, qr8, kcmp_s, vcmp_s, wk, wv, _pad_rows(kw_s, DEC_PAD),
                                       _pad_rows(vw_s, DEC_PAD), g8, past, n_cmp_s, dt)
        psum_s = psum_s.transpose(1, 0, 2, 3).reshape(N_KV, db * DEC_PAD, n_ch_s)
        selx = _select_sample(psum_s, mt_s, exp_s, past, n_slc_s, dt)
        need = (selx.reshape(N_KV, db, DEC_PAD, n_pages, page).max(axis=(0, 2, 4)) > 0.5).astype(jnp.int32)
        o_s8 = _slc_sample(page_table, need, qr8, selx, _pad_rows(ks_s, DEC_PAD), _pad_rows(vs_s, DEC_PAD), g8, part,
                           slc_k_n, slc_v_n, l, dt)
        o_s = jnp.pad(o_s8[:, :dt].reshape(ns_rows, QW), ((0, ROW_TILE - ns_rows), (0, 0)))

        x = _merge(x, c_p, c_s, o_cmp, o_slc, o_win, o_s, w_ga, w_gb, bf(w_conv_out[l]), bf(w_attn_out[l]),
                   bf(w_o[l]), row2(ln1_g[l]), row2(ln1_b[l]), alpha)
        x = _ffn(x, bf(w_gate[l]), bf(w_up[l]), bf(w_down[l]), row2(ln2_g[l]), row2(ln2_b[l]), alpha)

        prm = lambda a: a[:np_rows].reshape(n_seq, t_len, N_KV, HEAD_DIM)
        kv4 = lambda a: a.reshape(db, -1, N_KV, HEAD_DIM)
        n_keep = min(WINDOW, t_len)
        tail = lambda a, n: jnp.stack([a[(s + 1) * t_len - n:(s + 1) * t_len] for s in range(n_seq)], axis=0)
        tail_kv = lambda a: tail(a, n_keep).reshape(n_seq, n_keep, N_KV, HEAD_DIM)
        new_p = (tail(u, CONV_W - 1), prm(kc), prm(vc), prm(ks), prm(vs), tail_kv(kw), tail_kv(vw))
        new_s = (jnp.concatenate([state_conv[l], u_s], axis=1)[:, -(CONV_W - 1):],
                 kv4(smp(kc)), kv4(smp(vc)), kv4(ks_s), kv4(vs_s),
                 jnp.concatenate([state_win_k[l], kv4(kw_s)], axis=1)[:, dt:],
                 jnp.concatenate([state_win_v[l], kv4(vw_s)], axis=1)[:, dt:])
        for i in range(7):
            outs_p[i].append(new_p[i])
            outs_s[i].append(new_s[i])

    stk = lambda rows: jnp.stack(rows, axis=0)
    order = (1, 2, 3, 4, 5, 6, 0)
    return ((x[:np_rows].reshape(n_seq, t_len, d), x[np_rows:np_rows + ns_rows].reshape(db, dt, d))
            + tuple(stk(outs_p[i]) for i in order) + tuple(stk(outs_s[i]) for i in order))
```
